```python
import jax, jax.numpy as jnp
from jax import lax
import numpy as np

D_MODEL = 1024
BATCH = 32
SEQ = 2048
DEPTH = 1

MLSTM_HEADS = 4
MLSTM_HEAD_DIM = D_MODEL // 8
MLSTM_WIDTH = MLSTM_HEADS * MLSTM_HEAD_DIM
HGRN_HEADS = 4
HGRN_HEAD_DIM = D_MODEL // 8
HGRN_WIDTH = HGRN_HEADS * HGRN_HEAD_DIM
MIX_WIDTH = MLSTM_WIDTH + HGRN_WIDTH
CONV_WIDTH = 4
MLSTM_CHUNK = 64
HGRN_CHUNK = 16
D_FF = ((8 * D_MODEL // 3 + 255) // 256) * 256
FFN_RESIDUAL_WEIGHT = 0.5
EPS = 1e-6
N_MOD = 9
M_NEG = -1e30
PROJ_SPLITS = [MLSTM_WIDTH, MLSTM_WIDTH, MLSTM_WIDTH, MLSTM_WIDTH, 2 * MLSTM_HEADS,
               HGRN_WIDTH, HGRN_WIDTH, HGRN_WIDTH, HGRN_WIDTH]
PROJ_WIDTH = sum(PROJ_SPLITS)
PROJ_IDX = list(np.cumsum(PROJ_SPLITS)[:-1])

kernel_name = "hymba_mlstm_hgrn2_macaron_layer"


def rmsnorm(x, g):
    xf = x.astype(jnp.float32)
    y = xf * lax.rsqrt(jnp.mean(xf * xf, axis=-1, keepdims=True) + EPS)
    return (y * g.astype(jnp.float32)).astype(x.dtype)


def head_rmsnorm(h, g, n_heads):
    B, S, W = h.shape
    hf = h.astype(jnp.float32).reshape(B, S, n_heads, W // n_heads)
    hf = hf * lax.rsqrt(jnp.mean(hf * hf, axis=-1, keepdims=True) + EPS)
    return hf.reshape(B, S, W) * g.astype(jnp.float32)


def split_heads(a, n_heads):
    B, S, W = a.shape
    return a.reshape(B, S, n_heads, W // n_heads).transpose(0, 2, 1, 3)


def merge_heads(a):
    B, H, S, dh = a.shape
    return a.transpose(0, 2, 1, 3).reshape(B, S, H * dh)


def to_chunks(a, L):
    B, H, S = a.shape[:3]
    return jnp.moveaxis(a.reshape(B, H, S // L, L, *a.shape[3:]), 2, 0)


def from_chunks(a):
    nc, B, H, L, d = a.shape
    return jnp.moveaxis(a, 0, 2).reshape(B, H, nc * L, d)


def causal_dwconv(x, w, b):
    C = x.shape[-1]
    y = lax.conv_general_dilated(x, w.astype(x.dtype)[:, None, :], window_strides=(1,),
                                 padding=[(CONV_WIDTH - 1, 0)],
                                 dimension_numbers=('NWC', 'WIO', 'NWC'),
                                 feature_group_count=C)
    return y + b.astype(x.dtype)


def mlstm_chunkwise(q, k, v, log_i, log_f):
    B, H, S, dh = q.shape
    L = MLSTM_CHUNK
    causal = jnp.tril(jnp.ones((L, L), dtype=bool))
    xs = (to_chunks(q, L), to_chunks(k, L), to_chunks(v, L), to_chunks(log_i, L), to_chunks(log_f, L))

    def step(carry, inp):
        C, n, m = carry
        qb, kb, vb, li, lf = inp
        bcum = jnp.cumsum(lf, axis=-1)
        d = bcum[..., :, None] - bcum[..., None, :] + li[..., None, :]
        d = jnp.where(causal, d, -jnp.inf)
        inter = bcum + m[..., None]
        m_t = jnp.maximum(inter, jnp.max(d, axis=-1))
        w = jnp.exp(d - m_t[..., None])
        a = jnp.exp(inter - m_t)
        s = jnp.einsum('bhtd,bhsd->bhts', qb, kb) * w
        num = a[..., None] * jnp.einsum('bhtd,bhde->bhte', qb, C) + jnp.einsum('bhts,bhse->bhte', s, vb)
        den = a * jnp.einsum('bhtd,bhd->bht', qb, n) + jnp.sum(s, axis=-1)
        h = num / jnp.maximum(jnp.abs(den), jnp.exp(-m_t))[..., None]
        m_new = m_t[..., -1]
        g = jnp.exp(bcum[..., -1:] - bcum + li - m_new[..., None])
        decay = jnp.exp(bcum[..., -1] + m - m_new)
        C_new = decay[..., None, None] * C + jnp.einsum('bhs,bhsd,bhse->bhde', g, kb, vb)
        n_new = decay[..., None] * n + jnp.einsum('bhs,bhsd->bhd', g, kb)
        return (C_new, n_new, m_new), h

    init = (jnp.zeros((B, H, dh, dh), jnp.float32), jnp.zeros((B, H, dh), jnp.float32),
            jnp.full((B, H), M_NEG, jnp.float32))
    _, hs = lax.scan(step, init, xs)
    return from_chunks(hs)


def hgrn2_chunkwise(q, k, v, log_f):
    B, H, S, dk = q.shape
    dv = v.shape[-1]
    L = HGRN_CHUNK
    causal = jnp.tril(jnp.ones((L, L), dtype=bool))[:, :, None]
    xs = (to_chunks(q, L), to_chunks(k, L), to_chunks(v, L), to_chunks(log_f, L))

    def step(Sst, inp):
        qb, kb, vb, gb = inp
        bc = jnp.cumsum(gb, axis=2)
        inter = jnp.einsum('bhtc,bhce->bhte', qb * jnp.exp(bc), Sst)
        dec = jnp.exp(jnp.where(causal, bc[:, :, :, None, :] - bc[:, :, None, :, :], -jnp.inf))
        att = jnp.einsum('bhtc,bhsc,bhtsc->bhts', qb, kb, dec)
        o = inter + jnp.einsum('bhts,bhse->bhte', att, vb)
        S_new = jnp.exp(bc[:, :, -1])[..., None] * Sst + jnp.einsum(
            'bhsc,bhse->bhce', kb * jnp.exp(bc[:, :, -1:] - bc), vb)
        return S_new, o

    _, os = lax.scan(step, jnp.zeros((B, H, dk, dv), jnp.float32), xs)
    return from_chunks(os)


def swiglu(y, w_gate, w_up, w_down):
    return (jax.nn.silu(y @ w_gate) * (y @ w_up)) @ w_down


def token_mix(y, w_in, conv_w, conv_b, gate_b, mnorm_g, lb, hnorm_g, w_out):
    f32 = jnp.float32
    p = y @ w_in
    mq, mk, mv, mo, mif, hq, hf, hi, hg = jnp.split(p, PROJ_IDX, axis=-1)

    qk = jax.nn.silu(causal_dwconv(jnp.concatenate([mq, mk], axis=-1), conv_w, conv_b))
    mq, mk = jnp.split(qk, 2, axis=-1)
    q = split_heads(mq, MLSTM_HEADS).astype(f32)
    k = split_heads(mk, MLSTM_HEADS).astype(f32) * (MLSTM_HEAD_DIM ** -0.5)
    v = split_heads(mv, MLSTM_HEADS).astype(f32)
    gates = (mif.astype(f32) + gate_b.astype(f32)).transpose(0, 2, 1)
    log_i = gates[:, :MLSTM_HEADS]
    log_f = jax.nn.log_sigmoid(gates[:, MLSTM_HEADS:])
    hm = merge_heads(mlstm_chunkwise(q, k, v, log_i, log_f))
    out_m = head_rmsnorm(hm, mnorm_g, MLSTM_HEADS) * jax.nn.sigmoid(mo.astype(f32))

    fgate = lb.astype(f32) + (1.0 - lb.astype(f32)) * jax.nn.sigmoid(hf.astype(f32))
    hq_ = split_heads(jax.nn.silu(hq.astype(f32)), HGRN_HEADS)
    hk_ = split_heads(1.0 - fgate, HGRN_HEADS)
    hv_ = split_heads(hi.astype(f32), HGRN_HEADS)
    hlf = split_heads(jnp.log(fgate), HGRN_HEADS)
    ho = merge_heads(hgrn2_chunkwise(hq_, hk_, hv_, hlf))
    out_h = head_rmsnorm(ho, hnorm_g, HGRN_HEADS) * jax.nn.silu(hg.astype(f32))

    merged = jnp.concatenate([out_m, out_h], axis=-1).astype(y.dtype)
    return merged @ w_out


def setup_inputs(seed: int = 0) -> dict:
    key = jax.random.key(seed)
    ks = jax.random.split(key, 32)
    nrm = jax.random.normal
    D, F = D_MODEL, D_FF

    def gain(k, n):
        return 1.0 + 0.05 * nrm(k, (DEPTH, n), jnp.float32)

    forget_bias = jnp.linspace(3.0, 6.0, MLSTM_HEADS, dtype=jnp.float32)[None, :] \
        + 0.1 * nrm(ks[14], (DEPTH, MLSTM_HEADS), jnp.float32)
    input_bias = 0.1 * nrm(ks[15], (DEPTH, MLSTM_HEADS), jnp.float32)
    return {
        "x": nrm(ks[0], (BATCH, SEQ, D), jnp.float32),
        "c": nrm(ks[1], (BATCH, D), jnp.float32),
        "ada_w": 0.5 * (D ** -0.5) * nrm(ks[2], (DEPTH, D, N_MOD * D), jnp.float32),
        "ada_b": 0.02 * nrm(ks[3], (DEPTH, N_MOD * D), jnp.float32),
        "ffn1_pre_g": gain(ks[4], D),
        "ffn1_post_g": gain(ks[5], D),
        "ffn1_w_gate": (D ** -0.5) * nrm(ks[6], (DEPTH, D, F), jnp.float32),
        "ffn1_w_up": (D ** -0.5) * nrm(ks[7], (DEPTH, D, F), jnp.float32),
        "ffn1_w_down": (F ** -0.5) * nrm(ks[8], (DEPTH, F, D), jnp.float32),
        "mix_pre_g": gain(ks[9], D),
        "mix_post_g": gain(ks[10], D),
        "w_in": (D ** -0.5) * nrm(ks[11], (DEPTH, D, PROJ_WIDTH), jnp.float32),
        "mlstm_conv_w": (CONV_WIDTH ** -0.5) * nrm(ks[12], (DEPTH, CONV_WIDTH, 2 * MLSTM_WIDTH), jnp.float32),
        "mlstm_conv_b": 0.02 * nrm(ks[13], (DEPTH, 2 * MLSTM_WIDTH), jnp.float32),
        "mlstm_gate_b": jnp.concatenate([input_bias, forget_bias], axis=-1),
        "mlstm_norm_g": gain(ks[16], MLSTM_WIDTH),
        "hgrn_lb_logits": nrm(ks[17], (DEPTH + 1, HGRN_WIDTH), jnp.float32),
        "hgrn_norm_g": gain(ks[18], HGRN_WIDTH),
        "w_out": (MIX_WIDTH ** -0.5) * nrm(ks[19], (DEPTH, MIX_WIDTH, D), jnp.float32),
        "ffn2_pre_g": gain(ks[20], D),
        "ffn2_post_g": gain(ks[21], D),
        "ffn2_w_gate": (D ** -0.5) * nrm(ks[22], (DEPTH, D, F), jnp.float32),
        "ffn2_w_up": (D ** -0.5) * nrm(ks[23], (DEPTH, D, F), jnp.float32),
        "ffn2_w_down": (F ** -0.5) * nrm(ks[24], (DEPTH, F, D), jnp.float32),
    }


def reference(x, c, ada_w, ada_b, ffn1_pre_g, ffn1_post_g, ffn1_w_gate, ffn1_w_up, ffn1_w_down,
              mix_pre_g, mix_post_g, w_in, mlstm_conv_w, mlstm_conv_b, mlstm_gate_b, mlstm_norm_g,
              hgrn_lb_logits, hgrn_norm_g, w_out, ffn2_pre_g, ffn2_post_g, ffn2_w_gate, ffn2_w_up,
              ffn2_w_down):
    lb_all = jnp.cumsum(jax.nn.softmax(hgrn_lb_logits.astype(jnp.float32), axis=0), axis=0)
    sc = jax.nn.silu(c)
    h = x
    for l in range(DEPTH):
        mod = (sc @ ada_w[l] + ada_b[l])[:, None, :]
        (sh1, sc1, g1, sh2, sc2, g2, sh3, sc3, g3) = jnp.split(mod, N_MOD, axis=-1)

        y = rmsnorm(h, ffn1_pre_g[l]) * (1.0 + sc1) + sh1
        y = swiglu(y, ffn1_w_gate[l], ffn1_w_up[l], ffn1_w_down[l])
        h = h + FFN_RESIDUAL_WEIGHT * g1 * rmsnorm(y, ffn1_post_g[l])

        y = rmsnorm(h, mix_pre_g[l]) * (1.0 + sc2) + sh2
        y = token_mix(y, w_in[l], mlstm_conv_w[l], mlstm_conv_b[l], mlstm_gate_b[l], mlstm_norm_g[l],
                      lb_all[l], hgrn_norm_g[l], w_out[l])
        h = h + g2 * rmsnorm(y, mix_post_g[l])

        y = rmsnorm(h, ffn2_pre_g[l]) * (1.0 + sc3) + sh3
        y = swiglu(y, ffn2_w_gate[l], ffn2_w_up[l], ffn2_w_down[l])
        h = h + FFN_RESIDUAL_WEIGHT * g3 * rmsnorm(y, ffn2_post_g[l])
    return h
```

```python
import functools

import jax
import jax.numpy as jnp
from jax import lax
from jax.experimental import pallas as pl
from jax.experimental.pallas import tpu as pltpu

F32 = jnp.float32
BF16 = jnp.bfloat16

V7X_LANES = 128
V7X_SUBLANES = 8
V7X_MXU_WIDTH = 256
V7X_VMEM_BYTES = 64 * 1024 * 1024

EPS = 1e-6
M_NEG = -1e30
N_MOD = 9
HEADS = 4
HEAD_DIM = 128
GROUP_WIDTH = HEADS * HEAD_DIM
CONV_WIDTH = 4
FFN_RESIDUAL_WEIGHT = 0.5

FFN_TOKENS = 512
FFN_CHUNK = V7X_MXU_WIDTH
CHUNK = 128
LEVELS = 7
PROJ_COLS = 8 * GROUP_WIDTH
CONV_HALO = V7X_SUBLANES


def _sigmoid(x):
    return 1.0 / (1.0 + jnp.exp(-x))


def _silu(x):
    return x * _sigmoid(x)


def _log_sigmoid(x):
    return jnp.minimum(x, 0.0) - jnp.log(1.0 + jnp.exp(-jnp.abs(x)))


def _rms(x):
    return x * lax.rsqrt(jnp.mean(x * x, axis=-1, keepdims=True) + EPS)


def _dot(a, b):
    return jnp.dot(a, b, preferred_element_type=F32)


def _dot_nt(a, b):
    return lax.dot_general(a, b, (((1,), (1,)), ((), ())), preferred_element_type=F32)


def _dot_tn(a, b):
    return lax.dot_general(a, b, (((0,), (0,)), ((), ())), preferred_element_type=F32)


def _split3(x):
    p1 = x.astype(BF16)
    r1 = x - p1.astype(F32)
    p2 = r1.astype(BF16)
    p3 = (r1 - p2.astype(F32)).astype(BF16)
    return p1, p2, p3


def _vmem_limit(nbytes):
    return int(min(nbytes, V7X_VMEM_BYTES - 4 * 1024 * 1024))


def _mod_kernel(c_ref, w_ref, b_ref, o_ref):
    sc = _silu(c_ref[...]).astype(BF16)
    o_ref[...] = _dot(sc, w_ref[...].astype(BF16)) + b_ref[...]


def _adaln(c, w, b):
    bsz, d = c.shape
    n = w.shape[1]
    return pl.pallas_call(
        _mod_kernel,
        grid=(n // d,),
        in_specs=[
            pl.BlockSpec((bsz, d), lambda j: (0, 0)),
            pl.BlockSpec((d, d), lambda j: (0, j)),
            pl.BlockSpec((1, d), lambda j: (0, j)),
        ],
        out_specs=pl.BlockSpec((bsz, d), lambda j: (0, j)),
        out_shape=jax.ShapeDtypeStruct((bsz, n), F32),
        name="adaln_mod",
    )(c, w, b.reshape(1, n))


def _ffn_kernel(x_ref, mod_ref, pre_ref, post_ref, wg_ref, wu_ref, wd_ref, o_ref, *, row0):
    x = x_ref[...]
    y = _rms(x) * pre_ref[...]
    y = y * (1.0 + mod_ref[row0 + 1:row0 + 2, :]) + mod_ref[row0:row0 + 1, :]
    yb = y.astype(BF16)
    acc = jnp.zeros(x.shape, F32)
    for c in range(wg_ref.shape[0]):
        g = _dot(yb, wg_ref[c])
        u = _dot(yb, wu_ref[c])
        a = (_silu(g) * u).astype(BF16)
        acc = acc + _dot(a, wd_ref[c])
    z = _rms(acc) * post_ref[...]
    o_ref[...] = x + (FFN_RESIDUAL_WEIGHT * mod_ref[row0 + 2:row0 + 3, :]) * z


def _ffn(h, mod, pre_g, post_g, wg, wu, wd, *, row0, seq):
    n, d = h.shape
    tm = FFN_TOKENS
    tiles_per_seq = seq // tm
    nf, _, fc = wg.shape
    const3 = lambda i: (0, 0, 0)
    resident = pl.Buffered(1)
    est = (4 * tm * d * 4
           + 3 * nf * d * fc * 2
           + 6 * tm * d * 4)
    return pl.pallas_call(
        functools.partial(_ffn_kernel, row0=row0),
        grid=(n // tm,),
        in_specs=[
            pl.BlockSpec((tm, d), lambda i: (i, 0)),
            pl.BlockSpec((None, N_MOD, d), lambda i: (i // tiles_per_seq, 0, 0)),
            pl.BlockSpec((1, d), lambda i: (0, 0)),
            pl.BlockSpec((1, d), lambda i: (0, 0)),
            pl.BlockSpec((nf, d, fc), const3, pipeline_mode=resident),
            pl.BlockSpec((nf, d, fc), const3, pipeline_mode=resident),
            pl.BlockSpec((nf, fc, d), const3, pipeline_mode=resident),
        ],
        out_specs=pl.BlockSpec((tm, d), lambda i: (i, 0)),
        out_shape=jax.ShapeDtypeStruct((n, d), F32),
        compiler_params=pltpu.CompilerParams(
            dimension_semantics=("arbitrary",), vmem_limit_bytes=_vmem_limit(est)),
        name="ffn",
    )(h, mod, pre_g, post_g, wg, wu, wd)


def _proj_kernel(h_ref, mod_ref, pre_ref, wm_ref, wgc_ref, wgr_ref, p_ref, gc_ref, gr_ref, *, row0):
    y = _rms(h_ref[...]) * pre_ref[...]
    y = y * (1.0 + mod_ref[row0 + 1:row0 + 2, :]) + mod_ref[row0:row0 + 1, :]
    yb = y.astype(BF16)
    w = GROUP_WIDTH
    for c in range(PROJ_COLS // w):
        p_ref[:, c * w:(c + 1) * w] = _dot(yb, wm_ref[:, c * w:(c + 1) * w])
    gc_ref[...] = _dot(yb, wgc_ref[...])
    gr_ref[...] = _dot_nt(wgr_ref[...], yb)


def _proj(h, mod, pre_g, wm, wgc, wgr, *, row0, seq):
    n, d = h.shape
    tm = FFN_TOKENS
    tiles_per_seq = seq // tm
    resident = pl.Buffered(1)
    est = (2 * tm * d * 4 + 2 * tm * PROJ_COLS * 4 + d * PROJ_COLS * 2
           + 4 * tm * d * 4 + 8 * 1024 * 1024)
    return pl.pallas_call(
        functools.partial(_proj_kernel, row0=row0),
        grid=(n // tm,),
        in_specs=[
            pl.BlockSpec((tm, d), lambda i: (i, 0)),
            pl.BlockSpec((None, N_MOD, d), lambda i: (i // tiles_per_seq, 0, 0)),
            pl.BlockSpec((1, d), lambda i: (0, 0)),
            pl.BlockSpec((d, PROJ_COLS), lambda i: (0, 0), pipeline_mode=resident),
            pl.BlockSpec((d, V7X_LANES), lambda i: (0, 0)),
            pl.BlockSpec((V7X_SUBLANES, d), lambda i: (0, 0)),
        ],
        out_specs=[
            pl.BlockSpec((tm, PROJ_COLS), lambda i: (i, 0)),
            pl.BlockSpec((tm, V7X_LANES), lambda i: (i, 0)),
            pl.BlockSpec((V7X_SUBLANES, tm), lambda i: (0, i)),
        ],
        out_shape=[
            jax.ShapeDtypeStruct((n, PROJ_COLS), F32),
            jax.ShapeDtypeStruct((n, V7X_LANES), F32),
            jax.ShapeDtypeStruct((V7X_SUBLANES, n), F32),
        ],
        compiler_params=pltpu.CompilerParams(
            dimension_semantics=("arbitrary",), vmem_limit_bytes=_vmem_limit(est)),
        name="mixer_proj",
    )(h, mod, pre_g, wm, wgc, wgr)


def _mix_kernel(p_ref, gc_ref, gr_ref, cw_ref, cb_ref, gbc_ref, gbr_ref, mg_ref, lbl_ref, hng_ref,
                o_ref, conv_ref, c_ref, m_ref, s_ref, *, layer):
    t = CHUNK
    dh = HEAD_DIM
    gw = GROUP_WIDTH

    @pl.when(pl.program_id(1) == 0)
    def _():
        conv_ref[0:CONV_HALO, :] = jnp.zeros((CONV_HALO, 2 * gw), F32)
        c_ref[...] = jnp.zeros(c_ref.shape, F32)
        m_ref[...] = jnp.full(m_ref.shape, M_NEG, F32)
        s_ref[...] = jnp.zeros(s_ref.shape, F32)

    row = lax.broadcasted_iota(jnp.int32, (t, t), 0)
    col = lax.broadcasted_iota(jnp.int32, (t, t), 1)
    causal = col <= row
    tri_l = jnp.where(causal, 1.0, 0.0).astype(BF16)
    tri_u = jnp.where(row <= col, 1.0, 0.0).astype(BF16)
    xor = row ^ col
    level_masks = [(col < row) & ((xor >> l) == 1) for l in range(LEVELS)]
    diag_mask = row == col
    t_col = lax.broadcasted_iota(jnp.int32, (t, 1), 0)

    gc = gc_ref[...] + gbc_ref[...]
    gr = gr_ref[...] + gbr_ref[...]
    bc_c = sum(_dot(tri_l, part) for part in _split3(_log_sigmoid(gc)))
    bc_r = sum(_dot(part, tri_u) for part in _split3(_log_sigmoid(gr)))

    conv_ref[CONV_HALO:CONV_HALO + t, :] = p_ref[:, 0:2 * gw]
    xc = cb_ref[...]
    for j in range(CONV_WIDTH):
        start = CONV_HALO - (CONV_WIDTH - 1) + j
        xc = xc + cw_ref[j:j + 1, :] * conv_ref[start:start + t, :]
    conv_ref[0:CONV_HALO, :] = conv_ref[t:t + CONV_HALO, :]
    qk = _silu(xc)

    ones_b = jnp.ones((t, dh), BF16)
    for h in range(HEADS):
        q = qk[:, h * dh:(h + 1) * dh].astype(BF16)
        k = qk[:, gw + h * dh:gw + (h + 1) * dh] * (dh ** -0.5)
        v_aug = jnp.concatenate(
            [p_ref[:, 2 * gw + h * dh:2 * gw + (h + 1) * dh].astype(BF16), ones_b], axis=1)
        li_c = gc[:, h:h + 1]
        li_r = gr[h:h + 1, :]
        b_c = bc_c[:, HEADS + h:HEADS + h + 1]
        b_r = bc_r[HEADS + h:HEADS + h + 1, :]
        m_prev = m_ref[h:h + 1, 0:1]

        d = jnp.where(causal, b_c - b_r + li_r, -jnp.inf)
        inter = b_c + m_prev
        m_t = jnp.maximum(inter, jnp.max(d, axis=-1, keepdims=True))
        w = jnp.exp(d - m_t)
        a = jnp.exp(inter - m_t)
        s = _dot_nt(q, k.astype(BF16)) * w
        c_aug = c_ref[h]
        out = a * _dot(q, c_aug.astype(BF16)) + _dot(s.astype(BF16), v_aug)
        hm = out[:, 0:dh] / jnp.maximum(jnp.abs(out[:, dh:2 * dh]), jnp.exp(-m_t))

        b_last = b_c[t - 1:t, :]
        m_new = m_t[t - 1:t, :]
        g = jnp.exp(b_last - b_c + li_c - m_new)
        decay = jnp.exp(b_last + m_prev - m_new)
        c_ref[h] = decay * c_aug + _dot_tn((g * k).astype(BF16), v_aug)
        m_ref[h:h + 1, :] = jnp.broadcast_to(m_new, (1, V7X_LANES))

        gate = _sigmoid(p_ref[:, 3 * gw + h * dh:3 * gw + (h + 1) * dh])
        o_ref[:, h * dh:(h + 1) * dh] = _rms(hm) * mg_ref[:, h * dh:(h + 1) * dh] * gate

    lbl = lbl_ref[...]
    e_l = jnp.exp(lbl - jnp.max(lbl, axis=0, keepdims=True))
    lb = jnp.sum(e_l[0:layer + 1, :], axis=0, keepdims=True) / jnp.sum(e_l, axis=0, keepdims=True)
    fg = lb + (1.0 - lb) * _sigmoid(p_ref[:, 5 * gw:6 * gw])
    bc_all = sum(_dot(tri_l, part) for part in _split3(jnp.log(fg)))

    for h in range(HEADS):
        sl = slice(h * dh, (h + 1) * dh)
        qh = _silu(p_ref[:, 4 * gw + h * dh:4 * gw + (h + 1) * dh])
        kh = 1.0 - fg[:, sl]
        vb = p_ref[:, 6 * gw + h * dh:6 * gw + (h + 1) * dh].astype(BF16)
        bc = bc_all[:, sl]
        st = s_ref[h]

        att = jnp.where(diag_mask, _dot_nt(qh.astype(BF16), kh.astype(BF16)), 0.0)
        for l in range(LEVELS):
            half = 1 << l
            bc3 = bc.reshape(t // (2 * half), 2 * half, dh)
            anchor = jnp.broadcast_to(bc3[:, half - 1:half, :], bc3.shape).reshape(t, dh)
            upper = ((t_col >> l) & 1) == 1
            delta = bc - anchor
            e = jnp.exp(jnp.where(upper, delta, -delta))
            att = jnp.where(level_masks[l],
                            _dot_nt((qh * e).astype(BF16), (kh * e).astype(BF16)), att)

        o = _dot_nt((qh * jnp.exp(bc)).astype(BF16), st.astype(BF16)) + _dot(att.astype(BF16), vb)
        b_last = bc[t - 1:t, :]
        kd = (kh * jnp.exp(b_last - bc)).astype(BF16)
        s_ref[h] = jnp.exp(b_last) * st + _dot_tn(vb, kd)

        gate = _silu(p_ref[:, 7 * gw + h * dh:7 * gw + (h + 1) * dh])
        o_ref[:, gw + h * dh:gw + (h + 1) * dh] = _rms(o) * hng_ref[:, sl] * gate


def _mix(p, gc, gr, conv_w, conv_b, gate_b_col, gate_b_row, mnorm_g, lb_logits, hnorm_g,
         *, layer, batch, seq):
    n = p.shape[0]
    t = CHUNK
    chunks = seq // t
    gw = GROUP_WIDTH
    tok = lambda b, j: (b * chunks + j, 0)
    const = lambda b, j: (0, 0)
    return pl.pallas_call(
        functools.partial(_mix_kernel, layer=layer),
        grid=(batch, chunks),
        in_specs=[
            pl.BlockSpec((t, PROJ_COLS), tok),
            pl.BlockSpec((t, V7X_LANES), tok),
            pl.BlockSpec((V7X_SUBLANES, t), lambda b, j: (0, b * chunks + j)),
            pl.BlockSpec(conv_w.shape, const),
            pl.BlockSpec(conv_b.shape, const),
            pl.BlockSpec(gate_b_col.shape, const),
            pl.BlockSpec(gate_b_row.shape, const),
            pl.BlockSpec(mnorm_g.shape, const),
            pl.BlockSpec(lb_logits.shape, const),
            pl.BlockSpec(hnorm_g.shape, const),
        ],
        out_specs=pl.BlockSpec((t, 2 * gw), tok),
        out_shape=jax.ShapeDtypeStruct((n, 2 * gw), F32),
        scratch_shapes=[
            pltpu.VMEM((t + CONV_HALO, 2 * gw), F32),
            pltpu.VMEM((HEADS, HEAD_DIM, 2 * HEAD_DIM), F32),
            pltpu.VMEM((V7X_SUBLANES, V7X_LANES), F32),
            pltpu.VMEM((HEADS, HEAD_DIM, HEAD_DIM), F32),
        ],
        compiler_params=pltpu.CompilerParams(
            dimension_semantics=("arbitrary", "arbitrary"),
            vmem_limit_bytes=_vmem_limit(32 * 1024 * 1024)),
        name="mixer_recurrence",
    )(p, gc, gr, conv_w, conv_b, gate_b_col, gate_b_row, mnorm_g, lb_logits, hnorm_g)


def _out_kernel(h_ref, x_ref, mod_ref, post_ref, w_ref, o_ref, *, row0):
    y = _dot(x_ref[...].astype(BF16), w_ref[...])
    o_ref[...] = h_ref[...] + mod_ref[row0 + 2:row0 + 3, :] * (_rms(y) * post_ref[...])


def _out_proj(h, merged, mod, post_g, w, *, row0, seq):
    n, d = h.shape
    tm = FFN_TOKENS
    tiles_per_seq = seq // tm
    est = 6 * tm * d * 4 + 2 * d * d * 2 + 6 * tm * d * 4
    return pl.pallas_call(
        functools.partial(_out_kernel, row0=row0),
        grid=(n // tm,),
        in_specs=[
            pl.BlockSpec((tm, d), lambda i: (i, 0)),
            pl.BlockSpec((tm, merged.shape[1]), lambda i: (i, 0)),
            pl.BlockSpec((None, N_MOD, d), lambda i: (i // tiles_per_seq, 0, 0)),
            pl.BlockSpec((1, d), lambda i: (0, 0)),
            pl.BlockSpec(w.shape, lambda i: (0, 0)),
        ],
        out_specs=pl.BlockSpec((tm, d), lambda i: (i, 0)),
        out_shape=jax.ShapeDtypeStruct((n, d), F32),
        compiler_params=pltpu.CompilerParams(
            dimension_semantics=("arbitrary",), vmem_limit_bytes=_vmem_limit(est)),
        name="mixer_out",
    )(h, merged, mod, post_g, w)


def _ffn_weights(w_gate, w_up, w_down):
    d, f = w_gate.shape
    nf = f // FFN_CHUNK
    wg = w_gate.reshape(d, nf, FFN_CHUNK).transpose(1, 0, 2).astype(BF16)
    wu = w_up.reshape(d, nf, FFN_CHUNK).transpose(1, 0, 2).astype(BF16)
    wd = w_down.reshape(nf, FFN_CHUNK, d).astype(BF16)
    return wg, wu, wd


def kernel(x, c, ada_w, ada_b, ffn1_pre_g, ffn1_post_g, ffn1_w_gate, ffn1_w_up, ffn1_w_down, mix_pre_g, mix_post_g, w_in, mlstm_conv_w, mlstm_conv_b, mlstm_gate_b, mlstm_norm_g, hgrn_lb_logits, hgrn_norm_g, w_out, ffn2_pre_g, ffn2_post_g, ffn2_w_gate, ffn2_w_up, ffn2_w_down):
    batch, seq, d = x.shape
    n = batch * seq
    gw = GROUP_WIDTH
    depth = ada_w.shape[0]
    assert seq % FFN_TOKENS == 0 and seq % CHUNK == 0 and d % V7X_LANES == 0
    assert w_in.shape[2] == PROJ_COLS + 2 * HEADS

    h = x.reshape(n, d)
    for l in range(depth):
        mod = _adaln(c, ada_w[l], ada_b[l]).reshape(batch, N_MOD, d)

        h = _ffn(h, mod, ffn1_pre_g[l:l + 1], ffn1_post_g[l:l + 1],
                 *_ffn_weights(ffn1_w_gate[l], ffn1_w_up[l], ffn1_w_down[l]), row0=0, seq=seq)

        w = w_in[l]
        w_main = jnp.concatenate([w[:, :4 * gw], w[:, 4 * gw + 2 * HEADS:]], axis=1).astype(BF16)
        w_gates = w[:, 4 * gw:4 * gw + 2 * HEADS]
        w_gc = jnp.pad(w_gates, ((0, 0), (0, V7X_LANES - 2 * HEADS))).astype(BF16)
        w_gr = w_gates.T.astype(BF16)
        gate_b = mlstm_gate_b[l]
        gate_b_col = jnp.pad(gate_b, (0, V7X_LANES - 2 * HEADS)).reshape(1, V7X_LANES)
        gate_b_row = gate_b.reshape(2 * HEADS, 1)

        p, gc, gr = _proj(h, mod, mix_pre_g[l:l + 1], w_main, w_gc, w_gr, row0=3, seq=seq)
        merged = _mix(p, gc, gr, mlstm_conv_w[l], mlstm_conv_b[l:l + 1], gate_b_col, gate_b_row,
                      mlstm_norm_g[l:l + 1], hgrn_lb_logits, hgrn_norm_g[l:l + 1],
                      layer=l, batch=batch, seq=seq)
        h = _out_proj(h, merged, mod, mix_post_g[l:l + 1], w_out[l].astype(BF16), row0=3, seq=seq)

        h = _ffn(h, mod, ffn2_pre_g[l:l + 1], ffn2_post_g[l:l + 1],
                 *_ffn_weights(ffn2_w_gate[l], ffn2_w_up[l], ffn2_w_down[l]), row0=6, seq=seq)
    return h.reshape(batch, seq, d)
```

```python
import functools

import jax
import jax.numpy as jnp
from jax import lax
from jax.experimental import pallas as pl
from jax.experimental.pallas import tpu as pltpu

F32 = jnp.float32
BF16 = jnp.bfloat16

V7X_LANES = 128
V7X_SUBLANES = 8
V7X_MXU_WIDTH = 256
V7X_VMEM_BYTES = 64 * 1024 * 1024

EPS = 1e-6
M_NEG = -1e30
N_MOD = 9
HEADS = 4
HEAD_DIM = 128
GROUP_WIDTH = HEADS * HEAD_DIM
CONV_WIDTH = 4
FFN_RESIDUAL_WEIGHT = 0.5

FFN_TOKENS = 512
FFN_CHUNK = V7X_MXU_WIDTH
CHUNK = 128
CHUNKS_PER_STEP = 2
LEVELS = 7
PROJ_COLS = 8 * GROUP_WIDTH
CONV_HALO = V7X_SUBLANES


def _sigmoid(x):
    return 1.0 / (1.0 + jnp.exp(-x))


def _silu(x):
    return x * _sigmoid(x)


def _log_sigmoid(x):
    return jnp.minimum(x, 0.0) - jnp.log(1.0 + jnp.exp(-jnp.abs(x)))


def _rms(x):
    return x * lax.rsqrt(jnp.mean(x * x, axis=-1, keepdims=True) + EPS)


def _dot(a, b):
    return jnp.dot(a, b, preferred_element_type=F32)


def _dot_nt(a, b):
    return lax.dot_general(a, b, (((1,), (1,)), ((), ())), preferred_element_type=F32)


def _dot_tn(a, b):
    return lax.dot_general(a, b, (((0,), (0,)), ((), ())), preferred_element_type=F32)


def _split3(x):
    p1 = x.astype(BF16)
    r1 = x - p1.astype(F32)
    p2 = r1.astype(BF16)
    p3 = (r1 - p2.astype(F32)).astype(BF16)
    return p1, p2, p3


def _vmem_limit(nbytes):
    return int(min(nbytes, V7X_VMEM_BYTES - 4 * 1024 * 1024))


def _mod_kernel(c_ref, w_ref, b_ref, o_ref):
    sc = _silu(c_ref[...]).astype(BF16)
    o_ref[...] = _dot(sc, w_ref[...].astype(BF16)) + b_ref[...]


def _adaln(c, w, b):
    bsz, d = c.shape
    n = w.shape[1]
    return pl.pallas_call(
        _mod_kernel,
        grid=(n // d,),
        in_specs=[
            pl.BlockSpec((bsz, d), lambda j: (0, 0)),
            pl.BlockSpec((d, d), lambda j: (0, j)),
            pl.BlockSpec((1, d), lambda j: (0, j)),
        ],
        out_specs=pl.BlockSpec((bsz, d), lambda j: (0, j)),
        out_shape=jax.ShapeDtypeStruct((bsz, n), F32),
        name="adaln_mod",
    )(c, w, b.reshape(1, n))


def _ffn_kernel(x_ref, mod_ref, pre_ref, post_ref, wg_ref, wu_ref, wd_ref, o_ref, *, row0):
    x = x_ref[...]
    y = _rms(x) * pre_ref[...]
    y = y * (1.0 + mod_ref[row0 + 1:row0 + 2, :]) + mod_ref[row0:row0 + 1, :]
    yb = y.astype(BF16)
    acc = jnp.zeros(x.shape, F32)
    for c in range(wg_ref.shape[0]):
        g = _dot(yb, wg_ref[c])
        u = _dot(yb, wu_ref[c])
        a = (_silu(g) * u).astype(BF16)
        acc = acc + _dot(a, wd_ref[c])
    z = _rms(acc) * post_ref[...]
    o_ref[...] = x + (FFN_RESIDUAL_WEIGHT * mod_ref[row0 + 2:row0 + 3, :]) * z


def _ffn(h, mod, pre_g, post_g, wg, wu, wd, *, row0, seq):
    n, d = h.shape
    tm = FFN_TOKENS
    tiles_per_seq = seq // tm
    nf, _, fc = wg.shape
    const3 = lambda i: (0, 0, 0)
    resident = pl.Buffered(1)
    est = (4 * tm * d * 4
           + 3 * nf * d * fc * 2
           + 6 * tm * d * 4)
    return pl.pallas_call(
        functools.partial(_ffn_kernel, row0=row0),
        grid=(n // tm,),
        in_specs=[
            pl.BlockSpec((tm, d), lambda i: (i, 0)),
            pl.BlockSpec((None, N_MOD, d), lambda i: (i // tiles_per_seq, 0, 0)),
            pl.BlockSpec((1, d), lambda i: (0, 0)),
            pl.BlockSpec((1, d), lambda i: (0, 0)),
            pl.BlockSpec((nf, d, fc), const3, pipeline_mode=resident),
            pl.BlockSpec((nf, d, fc), const3, pipeline_mode=resident),
            pl.BlockSpec((nf, fc, d), const3, pipeline_mode=resident),
        ],
        out_specs=pl.BlockSpec((tm, d), lambda i: (i, 0)),
        out_shape=jax.ShapeDtypeStruct((n, d), F32),
        compiler_params=pltpu.CompilerParams(
            dimension_semantics=("arbitrary",), vmem_limit_bytes=_vmem_limit(est)),
        name="ffn",
    )(h, mod, pre_g, post_g, wg, wu, wd)


def _project_tasks(x, mod_ref, pre_ref, wm_ref, wgc_ref, wgr_ref, yb_ref, p_ref, gc_ref, gr_ref, row0):
    w = GROUP_WIDTH

    def piece(c):
        def run():
            if c == 0:
                y = _rms(x) * pre_ref[...]
                y = y * (1.0 + mod_ref[row0 + 1:row0 + 2, :]) + mod_ref[row0:row0 + 1, :]
                yb_ref[...] = y.astype(BF16)
            yb = yb_ref[...]
            p_ref[:, c * w:(c + 1) * w] = _dot(yb, wm_ref[:, c * w:(c + 1) * w])
            if c == PROJ_COLS // w - 1:
                gc_ref[...] = _dot(yb, wgc_ref[...])
                gr_ref[...] = _dot_nt(wgr_ref[...], yb)
        return run

    return [piece(c) for c in range(PROJ_COLS // w)]


def _chunk_constants():
    t = CHUNK
    row = lax.broadcasted_iota(jnp.int32, (t, t), 0)
    col = lax.broadcasted_iota(jnp.int32, (t, t), 1)
    causal = col <= row
    tri_l = jnp.where(causal, 1.0, 0.0).astype(BF16)
    tri_u = jnp.where(row <= col, 1.0, 0.0).astype(BF16)
    xor = row ^ col
    level_masks = [(col < row) & ((xor >> l) == 1) for l in range(LEVELS)]
    diag_mask = row == col
    t_col = lax.broadcasted_iota(jnp.int32, (t, 1), 0)
    upper = [((t_col >> l) & 1) == 1 for l in range(LEVELS)]
    return causal, tri_l, tri_u, level_masks, diag_mask, upper


def _recur_chunk(p_ref, gc_ref, gr_ref, consts, cw_ref, cb_ref, gbc_ref, gbr_ref, mg_ref, lbl_ref, hng_ref,
                 wo_ref, conv_ref, c_ref, m_ref, s_ref, layer, side_tasks):
    t = CHUNK
    dh = HEAD_DIM
    gw = GROUP_WIDTH
    heads = range(HEADS)
    causal, tri_l, tri_u, level_masks, diag_mask, upper = consts
    side = list(side_tasks)

    def run_side():
        if side:
            side.pop(0)()

    def head(x, h):
        return x[:, h * dh:(h + 1) * dh]

    y = None
    pending = None

    def emit(head_out, unit):
        nonlocal y, pending
        if pending is None:
            pending = head_out.astype(BF16)
            return
        pair = jnp.concatenate([pending, head_out.astype(BF16)], axis=1)
        part = _dot(pair, wo_ref[(unit - 1) * dh:(unit + 1) * dh, :])
        y = part if y is None else y + part
        pending = None

    gc = gc_ref[...] + gbc_ref[...]
    gr = gr_ref[...] + gbr_ref[...]
    bc_c = sum(_dot(tri_l, part) for part in _split3(_log_sigmoid(gc)))
    bc_r = sum(_dot(part, tri_u) for part in _split3(_log_sigmoid(gr)))
    run_side()

    conv_ref[CONV_HALO:CONV_HALO + t, :] = p_ref[:, 0:2 * gw]
    xc = cb_ref[...]
    for j in range(CONV_WIDTH):
        start = CONV_HALO - (CONV_WIDTH - 1) + j
        xc = xc + cw_ref[j:j + 1, :] * conv_ref[start:start + t, :]
    conv_ref[0:CONV_HALO, :] = conv_ref[t:t + CONV_HALO, :]
    qk = _silu(xc)
    q_b = [head(qk, h).astype(BF16) for h in heads]
    k_f = [head(qk, HEADS + h) * (dh ** -0.5) for h in heads]
    run_side()

    ones_b = jnp.ones((t, dh), BF16)
    v_aug = [jnp.concatenate([head(p_ref[:, 2 * gw:3 * gw], h).astype(BF16), ones_b], axis=1)
             for h in heads]
    li_c = [gc[:, h:h + 1] for h in heads]
    b_c = [bc_c[:, HEADS + h:HEADS + h + 1] for h in heads]
    m_prev = [m_ref[h:h + 1, 0:1] for h in heads]
    d = [jnp.where(causal, b_c[h] - bc_r[HEADS + h:HEADS + h + 1, :] + gr[h:h + 1, :], -jnp.inf)
         for h in heads]
    inter = [b_c[h] + m_prev[h] for h in heads]
    m_t = [jnp.maximum(inter[h], jnp.max(d[h], axis=-1, keepdims=True)) for h in heads]
    s = [(_dot_nt(q_b[h], k_f[h].astype(BF16)) * jnp.exp(d[h] - m_t[h])).astype(BF16) for h in heads]
    run_side()
    c_aug = [c_ref[h] for h in heads]
    out = [jnp.exp(inter[h] - m_t[h]) * _dot(q_b[h], c_aug[h].astype(BF16)) + _dot(s[h], v_aug[h])
           for h in heads]
    hm = [out[h][:, 0:dh] / jnp.maximum(jnp.abs(out[h][:, dh:2 * dh]), jnp.exp(-m_t[h])) for h in heads]
    for h in heads:
        b_last = b_c[h][t - 1:t, :]
        m_new = m_t[h][t - 1:t, :]
        g = jnp.exp(b_last - b_c[h] + li_c[h] - m_new)
        decay = jnp.exp(b_last + m_prev[h] - m_new)
        c_ref[h] = decay * c_aug[h] + _dot_tn((g * k_f[h]).astype(BF16), v_aug[h])
        m_ref[h:h + 1, :] = jnp.broadcast_to(m_new, (1, V7X_LANES))
    run_side()
    gate_m = _sigmoid(p_ref[:, 3 * gw:4 * gw])
    for h in heads:
        emit(_rms(hm[h]) * head(mg_ref[...], h) * head(gate_m, h), h)
    run_side()

    lbl = lbl_ref[...]
    e_l = jnp.exp(lbl - jnp.max(lbl, axis=0, keepdims=True))
    lb = jnp.sum(e_l[0:layer + 1, :], axis=0, keepdims=True) / jnp.sum(e_l, axis=0, keepdims=True)
    fg = lb + (1.0 - lb) * _sigmoid(p_ref[:, 5 * gw:6 * gw])
    bc = sum(_dot(tri_l, part) for part in _split3(jnp.log(fg)))
    qh = _silu(p_ref[:, 4 * gw:5 * gw])
    kh = 1.0 - fg
    v_b = p_ref[:, 6 * gw:7 * gw].astype(BF16)
    st = [s_ref[h] for h in heads]

    qh_b = qh.astype(BF16)
    kh_b = kh.astype(BF16)
    att = [jnp.where(diag_mask, _dot_nt(head(qh_b, h), head(kh_b, h)), 0.0) for h in heads]
    for l in range(LEVELS):
        half = 1 << l
        bc3 = bc.reshape(t // (2 * half), 2 * half, gw)
        anchor = jnp.broadcast_to(bc3[:, half - 1:half, :], bc3.shape).reshape(t, gw)
        e = jnp.exp(-jnp.abs(bc - anchor))
        z = (jnp.where(upper[l], qh, kh) * e).astype(BF16)
        att = [jnp.where(level_masks[l], _dot_nt(head(z, h), head(z, h)), att[h]) for h in heads]
        if l % 2 == 1:
            run_side()

    qe = (qh * jnp.exp(bc)).astype(BF16)
    o = [_dot_nt(head(qe, h), st[h].astype(BF16)) + _dot(att[h].astype(BF16), head(v_b, h))
         for h in heads]
    b_last = bc[t - 1:t, :]
    kd = (kh * jnp.exp(b_last - bc)).astype(BF16)
    dec = jnp.exp(b_last)
    for h in heads:
        s_ref[h] = head(dec, h) * st[h] + _dot_tn(head(v_b, h), head(kd, h))
    run_side()
    gate_h = _silu(p_ref[:, 7 * gw:8 * gw])
    for h in heads:
        emit(_rms(o[h]) * head(hng_ref[...], h) * head(gate_h, h), HEADS + h)
    while side:
        run_side()
    return y


def _mixer_kernel(h_ref, hn_ref, mod_ref, modn_ref, pre_ref, post_ref, wm_ref, wgc_ref, wgr_ref, wo_ref,
                  cw_ref, cb_ref, gbc_ref, gbr_ref, mg_ref, lbl_ref, hng_ref,
                  o_ref,
                  pa_ref, pb_ref, gca_ref, gcb_ref, gra_ref, grb_ref, yb_ref, conv_ref, c_ref, m_ref, s_ref,
                  *, layer, steps_per_seq, row0):
    t = CHUNK
    i = pl.program_id(0)
    bufs = ((pa_ref, gca_ref, gra_ref), (pb_ref, gcb_ref, grb_ref))

    def project_tasks(x, md_ref, buf):
        return _project_tasks(x, md_ref, pre_ref, wm_ref, wgc_ref, wgr_ref, yb_ref, *buf, row0)

    @pl.when(i == 0)
    def _():
        for task in project_tasks(h_ref[0:t, :], mod_ref, bufs[0]):
            task()

    @pl.when(i % steps_per_seq == 0)
    def _():
        conv_ref[0:CONV_HALO, :] = jnp.zeros((CONV_HALO, 2 * GROUP_WIDTH), F32)
        c_ref[...] = jnp.zeros(c_ref.shape, F32)
        m_ref[...] = jnp.full(m_ref.shape, M_NEG, F32)
        s_ref[...] = jnp.zeros(s_ref.shape, F32)

    gate_scale = mod_ref[row0 + 2:row0 + 3, :] * post_ref[...]
    consts = _chunk_constants()
    for k in range(CHUNKS_PER_STEP):
        rows = slice(k * t, (k + 1) * t)
        cur = bufs[k % 2]
        nxt = bufs[(k + 1) % 2]
        if k + 1 < CHUNKS_PER_STEP:
            side = project_tasks(h_ref[(k + 1) * t:(k + 2) * t, :], mod_ref, nxt)
        else:
            side = project_tasks(hn_ref[...], modn_ref, nxt)
        y = _recur_chunk(*cur, consts, cw_ref, cb_ref, gbc_ref, gbr_ref, mg_ref, lbl_ref, hng_ref,
                         wo_ref, conv_ref, c_ref, m_ref, s_ref, layer, side)
        o_ref[rows, :] = h_ref[rows, :] + gate_scale * _rms(y)


def _mixer(h, mod, pre_g, post_g, wm, wgc, wgr, wo, conv_w, conv_b, gate_b_col, gate_b_row,
           mnorm_g, lb_logits, hnorm_g, *, layer, row0, seq):
    n, d = h.shape
    t = CHUNK
    tm = CHUNKS_PER_STEP * t
    steps = n // tm
    steps_per_seq = seq // tm
    last_chunk = n // t - 1
    gw = GROUP_WIDTH
    resident = pl.Buffered(1)
    const = lambda i: (0, 0)
    nxt_chunk = lambda i: jnp.minimum(CHUNKS_PER_STEP * (i + 1), last_chunk)
    small = [conv_w, conv_b, gate_b_col, gate_b_row, mnorm_g, lb_logits, hnorm_g]
    est = (wm.size * 2 + wo.size * 2 + 2 * t * PROJ_COLS * 4
           + 6 * tm * d * 4 + 16 * 1024 * 1024)
    return pl.pallas_call(
        functools.partial(_mixer_kernel, layer=layer, steps_per_seq=steps_per_seq, row0=row0),
        grid=(steps,),
        in_specs=[
            pl.BlockSpec((tm, d), lambda i: (i, 0)),
            pl.BlockSpec((t, d), lambda i: (nxt_chunk(i), 0)),
            pl.BlockSpec((None, N_MOD, d), lambda i: (i // steps_per_seq, 0, 0)),
            pl.BlockSpec((None, N_MOD, d), lambda i: (nxt_chunk(i) // (seq // t), 0, 0)),
            pl.BlockSpec((1, d), const),
            pl.BlockSpec((1, d), const),
            pl.BlockSpec(wm.shape, const, pipeline_mode=resident),
            pl.BlockSpec(wgc.shape, const),
            pl.BlockSpec(wgr.shape, const),
            pl.BlockSpec(wo.shape, const, pipeline_mode=resident),
        ] + [pl.BlockSpec(a.shape, const) for a in small],
        out_specs=pl.BlockSpec((tm, d), lambda i: (i, 0)),
        out_shape=jax.ShapeDtypeStruct((n, d), F32),
        scratch_shapes=[
            pltpu.VMEM((t, PROJ_COLS), F32),
            pltpu.VMEM((t, PROJ_COLS), F32),
            pltpu.VMEM((t, V7X_LANES), F32),
            pltpu.VMEM((t, V7X_LANES), F32),
            pltpu.VMEM((V7X_SUBLANES, t), F32),
            pltpu.VMEM((V7X_SUBLANES, t), F32),
            pltpu.VMEM((t, d), BF16),
            pltpu.VMEM((t + CONV_HALO, 2 * gw), F32),
            pltpu.VMEM((HEADS, HEAD_DIM, 2 * HEAD_DIM), F32),
            pltpu.VMEM((V7X_SUBLANES, V7X_LANES), F32),
            pltpu.VMEM((HEADS, HEAD_DIM, HEAD_DIM), F32),
        ],
        compiler_params=pltpu.CompilerParams(
            dimension_semantics=("arbitrary",), vmem_limit_bytes=_vmem_limit(est)),
        name="mixer",
    )(h, h, mod, mod, pre_g, post_g, wm, wgc, wgr, wo, *small)


def _ffn_weights(w_gate, w_up, w_down):
    d, f = w_gate.shape
    nf = f // FFN_CHUNK
    wg = w_gate.reshape(d, nf, FFN_CHUNK).transpose(1, 0, 2).astype(BF16)
    wu = w_up.reshape(d, nf, FFN_CHUNK).transpose(1, 0, 2).astype(BF16)
    wd = w_down.reshape(nf, FFN_CHUNK, d).astype(BF16)
    return wg, wu, wd


def kernel(x, c, ada_w, ada_b, ffn1_pre_g, ffn1_post_g, ffn1_w_gate, ffn1_w_up, ffn1_w_down, mix_pre_g, mix_post_g, w_in, mlstm_conv_w, mlstm_conv_b, mlstm_gate_b, mlstm_norm_g, hgrn_lb_logits, hgrn_norm_g, w_out, ffn2_pre_g, ffn2_post_g, ffn2_w_gate, ffn2_w_up, ffn2_w_down):
    batch, seq, d = x.shape
    n = batch * seq
    gw = GROUP_WIDTH
    depth = ada_w.shape[0]
    assert seq % FFN_TOKENS == 0 and seq % (CHUNKS_PER_STEP * CHUNK) == 0 and d % V7X_LANES == 0
    assert w_in.shape[2] == PROJ_COLS + 2 * HEADS

    h = x.reshape(n, d)
    for l in range(depth):
        mod = _adaln(c, ada_w[l], ada_b[l]).reshape(batch, N_MOD, d)

        h = _ffn(h, mod, ffn1_pre_g[l:l + 1], ffn1_post_g[l:l + 1],
                 *_ffn_weights(ffn1_w_gate[l], ffn1_w_up[l], ffn1_w_down[l]), row0=0, seq=seq)

        w = w_in[l]
        w_main = jnp.concatenate([w[:, :4 * gw], w[:, 4 * gw + 2 * HEADS:]], axis=1).astype(BF16)
        w_gates = w[:, 4 * gw:4 * gw + 2 * HEADS]
        w_gc = jnp.pad(w_gates, ((0, 0), (0, V7X_LANES - 2 * HEADS))).astype(BF16)
        w_gr = w_gates.T.astype(BF16)
        gate_b = mlstm_gate_b[l]
        gate_b_col = jnp.pad(gate_b, (0, V7X_LANES - 2 * HEADS)).reshape(1, V7X_LANES)
        gate_b_row = gate_b.reshape(2 * HEADS, 1)

        h = _mixer(h, mod, mix_pre_g[l:l + 1], mix_post_g[l:l + 1], w_main, w_gc, w_gr,
                   w_out[l].astype(BF16), mlstm_conv_w[l], mlstm_conv_b[l:l + 1], gate_b_col, gate_b_row,
                   mlstm_norm_g[l:l + 1], hgrn_lb_logits, hgrn_norm_g[l:l + 1],
                   layer=l, row0=3, seq=seq)

        h = _ffn(h, mod, ffn2_pre_g[l:l + 1], ffn2_post_g[l:l + 1],
                 *_ffn_weights(ffn2_w_gate[l], ffn2_w_up[l], ffn2_w_down[l]), row0=6, seq=seq)
    return h.reshape(batch, seq, d)
```

```python
import functools

import numpy as np
import jax
import jax.numpy as jnp
from jax import lax
from jax.experimental import pallas as pl
from jax.experimental.pallas import tpu as pltpu

F32 = jnp.float32
BF16 = jnp.bfloat16

V7X_LANES = 128
V7X_SUBLANES = 8
V7X_MXU_WIDTH = 256
V7X_VMEM_BYTES = 64 * 1024 * 1024

EPS = 1e-6
M_NEG = -1e30
N_MOD = 9
HEADS = 4
HEAD_DIM = 128
GROUP_WIDTH = HEADS * HEAD_DIM
CONV_WIDTH = 4
FFN_RESIDUAL_WEIGHT = 0.5

FFN_TOKENS = 1024
FFN_PIECES = 4
FFN_CHUNK = V7X_MXU_WIDTH
CHUNK = 128
PROJ_ROWS = 2 * CHUNK
STEP_ROWS = 2 * PROJ_ROWS
LEVELS = 7
PROJ_COLS = 8 * GROUP_WIDTH
CONV_HALO = V7X_SUBLANES


def _sigmoid(x):
    return 1.0 / (1.0 + jnp.exp(-x))


def _silu(x):
    return x * _sigmoid(x)


def _log_sigmoid(x):
    return jnp.minimum(x, 0.0) - jnp.log(1.0 + jnp.exp(-jnp.abs(x)))


def _rms(x):
    return x * lax.rsqrt(jnp.mean(x * x, axis=-1, keepdims=True) + EPS)


def _dot(a, b):
    return jnp.dot(a, b, preferred_element_type=F32)


def _dot_nt(a, b):
    return lax.dot_general(a, b, (((1,), (1,)), ((), ())), preferred_element_type=F32)


def _dot_tn(a, b):
    return lax.dot_general(a, b, (((0,), (0,)), ((), ())), preferred_element_type=F32)


def _split3(x):
    p1 = x.astype(BF16)
    r1 = x - p1.astype(F32)
    p2 = r1.astype(BF16)
    p3 = (r1 - p2.astype(F32)).astype(BF16)
    return p1, p2, p3


def _vmem_limit(nbytes):
    return int(min(nbytes, V7X_VMEM_BYTES - 4 * 1024 * 1024))


def _mod_kernel(c_ref, w_ref, b_ref, o_ref):
    sc = _silu(c_ref[...]).astype(BF16)
    o_ref[...] = _dot(sc, w_ref[...].astype(BF16)) + b_ref[...]


def _adaln(c, w, b):
    bsz, d = c.shape
    n = w.shape[1]
    return pl.pallas_call(
        _mod_kernel,
        grid=(n // d,),
        in_specs=[
            pl.BlockSpec((bsz, d), lambda j: (0, 0)),
            pl.BlockSpec((d, d), lambda j: (0, j)),
            pl.BlockSpec((1, d), lambda j: (0, j)),
        ],
        out_specs=pl.BlockSpec((bsz, d), lambda j: (0, j)),
        out_shape=jax.ShapeDtypeStruct((bsz, n), F32),
        name="adaln_mod",
    )(c, w, b.reshape(1, n))


def _ffn_kernel(x_ref, mod_ref, pre_ref, post_ref, wg_ref, wu_ref, wd_ref, o_ref, yb_ref, acc_ref, *, row0):
    tm = x_ref.shape[0]
    half = tm // 2
    piece = half // FFN_PIECES
    in_scale = pre_ref[...] * (1.0 + mod_ref[row0 + 1:row0 + 2, :])
    in_shift = mod_ref[row0:row0 + 1, :]
    out_scale = (FFN_RESIDUAL_WEIGHT * mod_ref[row0 + 2:row0 + 3, :]) * post_ref[...]

    def prologue(r):
        rows = slice(r, r + piece)
        yb_ref[rows, :] = (_rms(x_ref[rows, :]) * in_scale + in_shift).astype(BF16)

    def epilogue(r):
        rows = slice(r, r + piece)
        o_ref[rows, :] = x_ref[rows, :] + out_scale * _rms(acc_ref[rows, :])

    def swiglu(r, side_tasks):
        side_tasks = list(side_tasks)
        yb = yb_ref[r:r + half, :]
        acc = None
        for c in range(wg_ref.shape[0]):
            g = _dot(yb, wg_ref[c])
            u = _dot(yb, wu_ref[c])
            part = _dot((_silu(g) * u).astype(BF16), wd_ref[c])
            acc = part if acc is None else acc + part
            if side_tasks:
                side_tasks.pop(0)()
        acc_ref[r:r + half, :] = acc

    pieces = [k * piece for k in range(FFN_PIECES)]
    for r in pieces:
        prologue(r)
    swiglu(0, [functools.partial(prologue, half + r) for r in pieces])
    swiglu(half, [functools.partial(epilogue, r) for r in pieces])
    for r in pieces:
        epilogue(half + r)


def _ffn(h, mod, pre_g, post_g, wg, wu, wd, *, row0, seq):
    n, d = h.shape
    tm = FFN_TOKENS
    tiles_per_seq = seq // tm
    nf, _, fc = wg.shape
    const3 = lambda i: (0, 0, 0)
    resident = pl.Buffered(1)
    est = (4 * tm * d * 4
           + 3 * nf * d * fc * 2
           + tm * d * (2 + 4)
           + 3 * tm * d * 4)
    return pl.pallas_call(
        functools.partial(_ffn_kernel, row0=row0),
        grid=(n // tm,),
        in_specs=[
            pl.BlockSpec((tm, d), lambda i: (i, 0)),
            pl.BlockSpec((None, N_MOD, d), lambda i: (i // tiles_per_seq, 0, 0)),
            pl.BlockSpec((1, d), lambda i: (0, 0)),
            pl.BlockSpec((1, d), lambda i: (0, 0)),
            pl.BlockSpec((nf, d, fc), const3, pipeline_mode=resident),
            pl.BlockSpec((nf, d, fc), const3, pipeline_mode=resident),
            pl.BlockSpec((nf, fc, d), const3, pipeline_mode=resident),
        ],
        out_specs=pl.BlockSpec((tm, d), lambda i: (i, 0)),
        out_shape=jax.ShapeDtypeStruct((n, d), F32),
        scratch_shapes=[pltpu.VMEM((tm, d), BF16),
                        pltpu.VMEM((tm, d), F32)],
        compiler_params=pltpu.CompilerParams(
            dimension_semantics=("arbitrary",), vmem_limit_bytes=_vmem_limit(est)),
        name="ffn",
    )(h, mod, pre_g, post_g, wg, wu, wd)


def _project_tasks(x, mod_ref, pre_ref, wm_ref, wgc_ref, wgr_ref, yb_ref, p_ref, gc_ref, gr_ref, row0):
    w = GROUP_WIDTH

    def piece(c):
        def run():
            if c == 0:
                y = _rms(x) * pre_ref[...]
                y = y * (1.0 + mod_ref[row0 + 1:row0 + 2, :]) + mod_ref[row0:row0 + 1, :]
                yb_ref[...] = y.astype(BF16)
            yb = yb_ref[...]
            p_ref[:, c * w:(c + 1) * w] = _dot(yb, wm_ref[c])
            if c == PROJ_COLS // w - 1:
                gc_ref[...] = _dot(yb, wgc_ref[...])
                gr_ref[...] = _dot_nt(wgr_ref[...], yb)
        return run

    return [piece(c) for c in range(PROJ_COLS // w)]


def _decay_selector():
    t = CHUNK
    tt, rr = np.meshgrid(np.arange(t), np.arange(t), indexing="ij")
    blocks = [rr <= tt, rr > tt]
    for l in range(1, LEVELS):
        half = 1 << l
        anchor = (tt & ~(2 * half - 1)) + half - 1
        is_upper = ((tt >> l) & 1) == 1
        blocks.append(np.where(is_upper, (rr > anchor) & (rr <= tt), (rr > tt) & (rr <= anchor)))
    sel = np.concatenate(blocks, axis=0).astype(np.float32)
    return jnp.asarray(np.concatenate([sel, sel], axis=1), dtype=BF16)


def _chunk_constants():
    t = CHUNK
    row = lax.broadcasted_iota(jnp.int32, (t, t), 0)
    col = lax.broadcasted_iota(jnp.int32, (t, t), 1)
    causal = col <= row
    tri_l = jnp.where(causal, 1.0, 0.0).astype(BF16)
    tri_u = jnp.where(row <= col, 1.0, 0.0).astype(BF16)
    xor = row ^ col
    level_masks = [(col < row) & ((xor >> l) == 1) for l in range(LEVELS)]
    diag_mask = row == col
    t_col = lax.broadcasted_iota(jnp.int32, (t, 1), 0)
    upper = [((t_col >> l) & 1) == 1 for l in range(LEVELS)]
    return causal, tri_l, tri_u, level_masks, diag_mask, upper


def _recur_chunk(p_ref, gc_ref, gr_ref, consts, cw_ref, cb_ref, gbc_ref, gbr_ref, mg_ref, lbl_ref, hng_ref,
                 sel_ref, wo_ref, conv_ref, c_ref, m_ref, s_ref, layer, side_tasks):
    t = CHUNK
    dh = HEAD_DIM
    gw = GROUP_WIDTH
    heads = range(HEADS)
    causal, tri_l, tri_u, level_masks, diag_mask, upper = consts
    parts = []

    def head(x, h):
        return x[:, h * dh:(h + 1) * dh]

    def project_out(outs, first_unit):
        for j in range(0, HEADS, 2):
            pair = jnp.concatenate([outs[j].astype(BF16), outs[j + 1].astype(BF16)], axis=1)
            u = first_unit + j
            parts.append(_dot(pair, wo_ref[u * dh:(u + 2) * dh, :]))

    def mlstm():
        gc = gc_ref[...] + gbc_ref[...]
        gr = gr_ref[...] + gbr_ref[...]
        bc_c = sum(_dot(tri_l, part) for part in _split3(_log_sigmoid(gc)))
        bc_r = sum(_dot(part, tri_u) for part in _split3(_log_sigmoid(gr)))
        yield
        conv_ref[CONV_HALO:CONV_HALO + t, :] = p_ref[:, 0:2 * gw]
        xc = cb_ref[...]
        for j in range(CONV_WIDTH):
            start = CONV_HALO - (CONV_WIDTH - 1) + j
            xc = xc + cw_ref[j:j + 1, :] * conv_ref[start:start + t, :]
        conv_ref[0:CONV_HALO, :] = conv_ref[t:t + CONV_HALO, :]
        qk = _silu(xc)
        q_b = [head(qk, h).astype(BF16) for h in heads]
        k_f = [head(qk, HEADS + h) * (dh ** -0.5) for h in heads]
        yield
        ones_b = jnp.ones((t, dh), BF16)
        v_aug = [jnp.concatenate([head(p_ref[:, 2 * gw:3 * gw], h).astype(BF16), ones_b], axis=1)
                 for h in heads]
        li_c = [gc[:, h:h + 1] for h in heads]
        b_c = [bc_c[:, HEADS + h:HEADS + h + 1] for h in heads]
        m_prev = [m_ref[h:h + 1, 0:1] for h in heads]
        d = [jnp.where(causal, b_c[h] - bc_r[HEADS + h:HEADS + h + 1, :] + gr[h:h + 1, :], -jnp.inf)
             for h in heads]
        inter = [b_c[h] + m_prev[h] for h in heads]
        m_t = [jnp.maximum(inter[h], jnp.max(d[h], axis=-1, keepdims=True)) for h in heads]
        s = [(_dot_nt(q_b[h], k_f[h].astype(BF16)) * jnp.exp(d[h] - m_t[h])).astype(BF16)
             for h in heads]
        yield
        c_aug = [c_ref[h] for h in heads]
        out = [jnp.exp(inter[h] - m_t[h]) * _dot(q_b[h], c_aug[h].astype(BF16)) + _dot(s[h], v_aug[h])
               for h in heads]
        hm = [out[h][:, 0:dh] / jnp.maximum(jnp.abs(out[h][:, dh:2 * dh]), jnp.exp(-m_t[h]))
              for h in heads]
        yield
        for h in heads:
            b_last = b_c[h][t - 1:t, :]
            m_new = m_t[h][t - 1:t, :]
            g = jnp.exp(b_last - b_c[h] + li_c[h] - m_new)
            decay = jnp.exp(b_last + m_prev[h] - m_new)
            c_ref[h] = decay * c_aug[h] + _dot_tn((g * k_f[h]).astype(BF16), v_aug[h])
            m_ref[h:h + 1, :] = jnp.broadcast_to(m_new, (1, V7X_LANES))
        yield
        gate_m = _sigmoid(p_ref[:, 3 * gw:4 * gw])
        project_out([_rms(hm[h]) * head(mg_ref[...], h) * head(gate_m, h) for h in heads], 0)

    def hgrn():
        lbl = lbl_ref[...]
        e_l = jnp.exp(lbl - jnp.max(lbl, axis=0, keepdims=True))
        lb = jnp.sum(e_l[0:layer + 1, :], axis=0, keepdims=True) / jnp.sum(e_l, axis=0, keepdims=True)
        fg = lb + (1.0 - lb) * _sigmoid(p_ref[:, 5 * gw:6 * gw])
        lf = jnp.log(fg)
        lf_hi = lf.astype(BF16)
        lf_lo = (lf - lf_hi.astype(F32)).astype(BF16)
        logdec = _dot(sel_ref[...], jnp.concatenate([lf_hi, lf_lo], axis=0))
        yield
        qh = _silu(p_ref[:, 4 * gw:5 * gw])
        kh = 1.0 - fg
        qh_b = qh.astype(BF16)
        kh_b = kh.astype(BF16)
        att = [jnp.where(diag_mask, _dot_nt(head(qh_b, h), head(kh_b, h)), 0.0) for h in heads]
        for l in range(LEVELS):
            if l == 0:
                z = jnp.where(upper[0], qh * fg, kh).astype(BF16)
            else:
                e = jnp.exp(logdec[(l + 1) * t:(l + 2) * t, :])
                z = (jnp.where(upper[l], qh, kh) * e).astype(BF16)
            att = [jnp.where(level_masks[l], _dot_nt(head(z, h), head(z, h)), att[h]) for h in heads]
            yield
        bc = logdec[0:t, :]
        v_b = p_ref[:, 6 * gw:7 * gw].astype(BF16)
        st = [s_ref[h] for h in heads]
        qe = (qh * jnp.exp(bc)).astype(BF16)
        o = [_dot_nt(head(qe, h), st[h].astype(BF16)) + _dot(att[h].astype(BF16), head(v_b, h))
             for h in heads]
        yield
        kd = (kh * jnp.exp(logdec[t:2 * t, :])).astype(BF16)
        dec = jnp.exp(bc[t - 1:t, :])
        for h in heads:
            s_ref[h] = head(dec, h) * st[h] + _dot_tn(head(v_b, h), head(kd, h))
        yield
        gate_h = _silu(p_ref[:, 7 * gw:8 * gw])
        project_out([_rms(o[h]) * head(hng_ref[...], h) * head(gate_h, h) for h in heads], HEADS)

    def side():
        for task in side_tasks:
            task()
            yield

    streams = [side(), hgrn(), mlstm()]
    while streams:
        for stream in list(streams):
            if next(stream, streams) is streams:
                streams.remove(stream)
    return functools.reduce(lambda a, b: a + b, parts)


def _mixer_kernel(h_ref, hn_ref, mod_ref, modn_ref, pre_ref, post_ref, wm_ref, wgc_ref, wgr_ref, wo_ref,
                  cw_ref, cb_ref, gbc_ref, gbr_ref, mg_ref, lbl_ref, hng_ref, sel_ref,
                  o_ref,
                  pa_ref, pb_ref, gca_ref, gcb_ref, gra_ref, grb_ref, yb_ref, conv_ref, c_ref, m_ref, s_ref,
                  *, layer, steps_per_seq, row0):
    t = CHUNK
    pr = PROJ_ROWS
    i = pl.program_id(0)
    bufs = ((pa_ref, gca_ref, gra_ref), (pb_ref, gcb_ref, grb_ref))

    def project_tasks(x, md_ref, buf):
        return _project_tasks(x, md_ref, pre_ref, wm_ref, wgc_ref, wgr_ref, yb_ref, *buf, row0)

    @pl.when(i == 0)
    def _():
        for task in project_tasks(h_ref[0:pr, :], mod_ref, bufs[0]):
            task()

    @pl.when(i % steps_per_seq == 0)
    def _():
        conv_ref[0:CONV_HALO, :] = jnp.zeros((CONV_HALO, 2 * GROUP_WIDTH), F32)
        c_ref[...] = jnp.zeros(c_ref.shape, F32)
        m_ref[...] = jnp.full(m_ref.shape, M_NEG, F32)
        s_ref[...] = jnp.zeros(s_ref.shape, F32)

    gate_scale = mod_ref[row0 + 2:row0 + 3, :] * post_ref[...]
    consts = _chunk_constants()
    chunks_per_proj = pr // t
    for half in range(STEP_ROWS // pr):
        p_ref, gc_ref, gr_ref = bufs[half % 2]
        if (half + 1) * pr < STEP_ROWS:
            side = project_tasks(h_ref[(half + 1) * pr:(half + 2) * pr, :], mod_ref, bufs[(half + 1) % 2])
        else:
            side = project_tasks(hn_ref[...], modn_ref, bufs[(half + 1) % 2])
        per_chunk = len(side) // chunks_per_proj
        for k in range(chunks_per_proj):
            rows = slice(half * pr + k * t, half * pr + (k + 1) * t)
            y = _recur_chunk(p_ref.at[k * t:(k + 1) * t, :], gc_ref.at[k * t:(k + 1) * t, :],
                             gr_ref.at[:, k * t:(k + 1) * t], consts,
                             cw_ref, cb_ref, gbc_ref, gbr_ref, mg_ref, lbl_ref, hng_ref,
                             sel_ref, wo_ref, conv_ref, c_ref, m_ref, s_ref, layer,
                             side[k * per_chunk:(k + 1) * per_chunk])
            o_ref[rows, :] = h_ref[rows, :] + gate_scale * _rms(y)


def _mixer(h, mod, pre_g, post_g, wm, wgc, wgr, wo, conv_w, conv_b, gate_b_col, gate_b_row,
           mnorm_g, lb_logits, hnorm_g, *, layer, row0, seq):
    n, d = h.shape
    tm = STEP_ROWS
    pr = PROJ_ROWS
    steps = n // tm
    steps_per_seq = seq // tm
    last_proj = n // pr - 1
    gw = GROUP_WIDTH
    resident = pl.Buffered(1)
    const = lambda i: (0, 0)
    nxt_proj = lambda i: jnp.minimum((tm // pr) * (i + 1), last_proj)
    small = [conv_w, conv_b, gate_b_col, gate_b_row, mnorm_g, lb_logits, hnorm_g, _decay_selector()]
    est = (wm.size * 2 + wo.size * 2 + 2 * pr * PROJ_COLS * 4
           + 6 * tm * d * 4 + 16 * 1024 * 1024)
    return pl.pallas_call(
        functools.partial(_mixer_kernel, layer=layer, steps_per_seq=steps_per_seq, row0=row0),
        grid=(steps,),
        in_specs=[
            pl.BlockSpec((tm, d), lambda i: (i, 0)),
            pl.BlockSpec((pr, d), lambda i: (nxt_proj(i), 0)),
            pl.BlockSpec((None, N_MOD, d), lambda i: (i // steps_per_seq, 0, 0)),
            pl.BlockSpec((None, N_MOD, d), lambda i: (nxt_proj(i) // (seq // pr), 0, 0)),
            pl.BlockSpec((1, d), const),
            pl.BlockSpec((1, d), const),
            pl.BlockSpec(wm.shape, lambda i: (0, 0, 0), pipeline_mode=resident),
            pl.BlockSpec(wgc.shape, const),
            pl.BlockSpec(wgr.shape, const),
            pl.BlockSpec(wo.shape, const, pipeline_mode=resident),
        ] + [pl.BlockSpec(a.shape, const) for a in small],
        out_specs=pl.BlockSpec((tm, d), lambda i: (i, 0)),
        out_shape=jax.ShapeDtypeStruct((n, d), F32),
        scratch_shapes=[
            pltpu.VMEM((pr, PROJ_COLS), F32),
            pltpu.VMEM((pr, PROJ_COLS), F32),
            pltpu.VMEM((pr, V7X_LANES), F32),
            pltpu.VMEM((pr, V7X_LANES), F32),
            pltpu.VMEM((V7X_SUBLANES, pr), F32),
            pltpu.VMEM((V7X_SUBLANES, pr), F32),
            pltpu.VMEM((pr, d), BF16),
            pltpu.VMEM((CHUNK + CONV_HALO, 2 * gw), F32),
            pltpu.VMEM((HEADS, HEAD_DIM, 2 * HEAD_DIM), F32),
            pltpu.VMEM((V7X_SUBLANES, V7X_LANES), F32),
            pltpu.VMEM((HEADS, HEAD_DIM, HEAD_DIM), F32),
        ],
        compiler_params=pltpu.CompilerParams(
            dimension_semantics=("arbitrary",), vmem_limit_bytes=_vmem_limit(est)),
        name="mixer",
    )(h, h, mod, mod, pre_g, post_g, wm, wgc, wgr, wo, *small)


def _ffn_weights(w_gate, w_up, w_down):
    d, f = w_gate.shape
    nf = f // FFN_CHUNK
    wg = w_gate.reshape(d, nf, FFN_CHUNK).transpose(1, 0, 2).astype(BF16)
    wu = w_up.reshape(d, nf, FFN_CHUNK).transpose(1, 0, 2).astype(BF16)
    wd = w_down.reshape(nf, FFN_CHUNK, d).astype(BF16)
    return wg, wu, wd


def kernel(x, c, ada_w, ada_b, ffn1_pre_g, ffn1_post_g, ffn1_w_gate, ffn1_w_up, ffn1_w_down, mix_pre_g, mix_post_g, w_in, mlstm_conv_w, mlstm_conv_b, mlstm_gate_b, mlstm_norm_g, hgrn_lb_logits, hgrn_norm_g, w_out, ffn2_pre_g, ffn2_post_g, ffn2_w_gate, ffn2_w_up, ffn2_w_down):
    batch, seq, d = x.shape
    n = batch * seq
    gw = GROUP_WIDTH
    depth = ada_w.shape[0]
    assert seq % FFN_TOKENS == 0 and seq % STEP_ROWS == 0 and d % V7X_LANES == 0
    assert w_in.shape[2] == PROJ_COLS + 2 * HEADS

    h = x.reshape(n, d)
    for l in range(depth):
        mod = _adaln(c, ada_w[l], ada_b[l]).reshape(batch, N_MOD, d)

        h = _ffn(h, mod, ffn1_pre_g[l:l + 1], ffn1_post_g[l:l + 1],
                 *_ffn_weights(ffn1_w_gate[l], ffn1_w_up[l], ffn1_w_down[l]), row0=0, seq=seq)

        w = w_in[l]
        w_main = jnp.concatenate([w[:, :4 * gw], w[:, 4 * gw + 2 * HEADS:]], axis=1).astype(BF16)
        w_main = w_main.reshape(d, PROJ_COLS // gw, gw).transpose(1, 0, 2)
        w_gates = w[:, 4 * gw:4 * gw + 2 * HEADS]
        w_gc = jnp.pad(w_gates, ((0, 0), (0, V7X_LANES - 2 * HEADS))).astype(BF16)
        w_gr = w_gates.T.astype(BF16)
        gate_b = mlstm_gate_b[l]
        gate_b_col = jnp.pad(gate_b, (0, V7X_LANES - 2 * HEADS)).reshape(1, V7X_LANES)
        gate_b_row = gate_b.reshape(2 * HEADS, 1)

        h = _mixer(h, mod, mix_pre_g[l:l + 1], mix_post_g[l:l + 1], w_main, w_gc, w_gr,
                   w_out[l].astype(BF16), mlstm_conv_w[l], mlstm_conv_b[l:l + 1], gate_b_col, gate_b_row,
                   mlstm_norm_g[l:l + 1], hgrn_lb_logits, hgrn_norm_g[l:l + 1],
                   layer=l, row0=3, seq=seq)

        h = _ffn(h, mod, ffn2_pre_g[l:l + 1], ffn2_post_g[l:l + 1],
                 *_ffn_weights(ffn2_w_gate[l], ffn2_w_up[l], ffn2_w_down[l]), row0=6, seq=seq)
    return h.reshape(batch, seq, d)
```

```python
import functools

import numpy as np
import jax
import jax.numpy as jnp
from jax import lax
from jax.experimental import pallas as pl
from jax.experimental.pallas import tpu as pltpu

F32 = jnp.float32
BF16 = jnp.bfloat16

V7X_LANES = 128
V7X_SUBLANES = 8
V7X_MXU_WIDTH = 256
V7X_VMEM_BYTES = 64 * 1024 * 1024

EPS = 1e-6
M_NEG = -1e30
N_MOD = 9
HEADS = 4
HEAD_DIM = 128
GROUP_WIDTH = HEADS * HEAD_DIM
CONV_WIDTH = 4
FFN_RESIDUAL_WEIGHT = 0.5

FFN_TOKENS = 1024
FFN_PIECES = 4
FFN_CHUNK = V7X_MXU_WIDTH
CHUNK = 128
PROJ_ROWS = 2 * CHUNK
STEP_ROWS = 2 * PROJ_ROWS
LEVELS = 7
PROJ_COLS = 8 * GROUP_WIDTH
CONV_HALO = V7X_SUBLANES


def _sigmoid(x):
    return 1.0 / (1.0 + jnp.exp(-x))


def _silu(x):
    return x * _sigmoid(x)


def _log_sigmoid(x):
    return jnp.minimum(x, 0.0) - jnp.log(1.0 + jnp.exp(-jnp.abs(x)))


def _rms(x):
    return x * lax.rsqrt(jnp.mean(x * x, axis=-1, keepdims=True) + EPS)


def _dot(a, b):
    return jnp.dot(a, b, preferred_element_type=F32)


def _dot_nt(a, b):
    return lax.dot_general(a, b, (((1,), (1,)), ((), ())), preferred_element_type=F32)


def _dot_tn(a, b):
    return lax.dot_general(a, b, (((0,), (0,)), ((), ())), preferred_element_type=F32)


def _split3(x):
    p1 = x.astype(BF16)
    r1 = x - p1.astype(F32)
    p2 = r1.astype(BF16)
    p3 = (r1 - p2.astype(F32)).astype(BF16)
    return p1, p2, p3


def _vmem_limit(nbytes):
    return int(min(nbytes, V7X_VMEM_BYTES - 4 * 1024 * 1024))


def _mod_kernel(c_ref, w_ref, b_ref, o_ref):
    sc = _silu(c_ref[...]).astype(BF16)
    o_ref[...] = _dot(sc, w_ref[...].astype(BF16)) + b_ref[...]


def _adaln(c, w, b):
    bsz, d = c.shape
    n = w.shape[1]
    return pl.pallas_call(
        _mod_kernel,
        grid=(n // d,),
        in_specs=[
            pl.BlockSpec((bsz, d), lambda j: (0, 0)),
            pl.BlockSpec((d, d), lambda j: (0, j)),
            pl.BlockSpec((1, d), lambda j: (0, j)),
        ],
        out_specs=pl.BlockSpec((bsz, d), lambda j: (0, j)),
        out_shape=jax.ShapeDtypeStruct((bsz, n), F32),
        name="adaln_mod",
    )(c, w, b.reshape(1, n))


def _ffn_kernel(x_ref, mod_ref, pre_ref, post_ref, wg_ref, wu_ref, wd_ref, o_ref, yb_ref, acc_ref, *, row0):
    tm = x_ref.shape[0]
    half = tm // 2
    piece = half // FFN_PIECES
    in_scale = pre_ref[...] * (1.0 + mod_ref[row0 + 1:row0 + 2, :])
    in_shift = mod_ref[row0:row0 + 1, :]
    out_scale = (FFN_RESIDUAL_WEIGHT * mod_ref[row0 + 2:row0 + 3, :]) * post_ref[...]

    def prologue(r):
        rows = slice(r, r + piece)
        yb_ref[rows, :] = (_rms(x_ref[rows, :]) * in_scale + in_shift).astype(BF16)

    def epilogue(r):
        rows = slice(r, r + piece)
        o_ref[rows, :] = x_ref[rows, :] + out_scale * _rms(acc_ref[rows, :])

    def swiglu(r, side_tasks):
        side_tasks = list(side_tasks)
        yb = yb_ref[r:r + half, :]
        acc = None
        for c in range(wd_ref.shape[0]):
            cols = slice(c * FFN_CHUNK, (c + 1) * FFN_CHUNK)
            g = _dot(yb, wg_ref[:, cols])
            u = _dot(yb, wu_ref[:, cols])
            part = _dot((_silu(g) * u).astype(BF16), wd_ref[c])
            acc = part if acc is None else acc + part
            if side_tasks:
                side_tasks.pop(0)()
        acc_ref[r:r + half, :] = acc

    pieces = [k * piece for k in range(FFN_PIECES)]
    for r in pieces:
        prologue(r)
    swiglu(0, [functools.partial(prologue, half + r) for r in pieces])
    swiglu(half, [functools.partial(epilogue, r) for r in pieces])
    for r in pieces:
        epilogue(half + r)


def _ffn(h, mod, pre_g, post_g, wg, wu, wd, *, row0, seq):
    n, d = h.shape
    tm = FFN_TOKENS
    tiles_per_seq = seq // tm
    nf, fc, _ = wd.shape
    const3 = lambda i: (0, 0, 0)
    resident = pl.Buffered(1)
    est = (4 * tm * d * 4
           + 3 * nf * d * fc * 2
           + tm * d * (2 + 4)
           + 3 * tm * d * 4)
    return pl.pallas_call(
        functools.partial(_ffn_kernel, row0=row0),
        grid=(n // tm,),
        in_specs=[
            pl.BlockSpec((tm, d), lambda i: (i, 0)),
            pl.BlockSpec((None, N_MOD, d), lambda i: (i // tiles_per_seq, 0, 0)),
            pl.BlockSpec((1, d), lambda i: (0, 0)),
            pl.BlockSpec((1, d), lambda i: (0, 0)),
            pl.BlockSpec(wg.shape, lambda i: (0, 0), pipeline_mode=resident),
            pl.BlockSpec(wu.shape, lambda i: (0, 0), pipeline_mode=resident),
            pl.BlockSpec((nf, fc, d), const3, pipeline_mode=resident),
        ],
        out_specs=pl.BlockSpec((tm, d), lambda i: (i, 0)),
        out_shape=jax.ShapeDtypeStruct((n, d), F32),
        scratch_shapes=[pltpu.VMEM((tm, d), BF16),
                        pltpu.VMEM((tm, d), F32)],
        compiler_params=pltpu.CompilerParams(
            dimension_semantics=("arbitrary",), vmem_limit_bytes=_vmem_limit(est)),
        name="ffn",
    )(h, mod, pre_g, post_g, wg, wu, wd)


def _project_tasks(x, mod_ref, pre_ref, wm_ref, wgc_ref, yb_ref, p_ref, gc_ref, row0):
    w = GROUP_WIDTH

    def piece(c):
        def run():
            if c == 0:
                y = _rms(x) * pre_ref[...]
                y = y * (1.0 + mod_ref[row0 + 1:row0 + 2, :]) + mod_ref[row0:row0 + 1, :]
                yb_ref[...] = y.astype(BF16)
            yb = yb_ref[...]
            p_ref[:, c * w:(c + 1) * w] = _dot(yb, wm_ref[c])
            if c == PROJ_COLS // w - 1:
                gc_ref[...] = _dot(yb, wgc_ref[...])
        return run

    return [piece(c) for c in range(PROJ_COLS // w)]


def _decay_selector():
    t = CHUNK
    tt, rr = np.meshgrid(np.arange(t), np.arange(t), indexing="ij")
    blocks = [rr <= tt, rr > tt]
    for l in range(1, LEVELS):
        half = 1 << l
        anchor = (tt & ~(2 * half - 1)) + half - 1
        is_upper = ((tt >> l) & 1) == 1
        blocks.append(np.where(is_upper, (rr > anchor) & (rr <= tt), (rr > tt) & (rr <= anchor)))
    sel = np.concatenate(blocks, axis=0).astype(np.float32)
    return jnp.asarray(np.concatenate([sel, sel], axis=1), dtype=BF16)


def _chunk_constants():
    t = CHUNK
    row = lax.broadcasted_iota(jnp.int32, (t, t), 0)
    col = lax.broadcasted_iota(jnp.int32, (t, t), 1)
    causal = col <= row
    tri_l = jnp.where(causal, 1.0, 0.0).astype(BF16)
    xor = row ^ col
    level_masks = [(col < row) & ((xor >> l) == 1) for l in range(LEVELS)]
    diag_mask = row == col
    t_col = lax.broadcasted_iota(jnp.int32, (t, 1), 0)
    upper = [((t_col >> l) & 1) == 1 for l in range(LEVELS)]
    return causal, tri_l, level_masks, diag_mask, upper


def _recur_chunk(p_ref, gc_ref, consts, cw_ref, cb_ref, gbc_ref, mg_ref, lbl_ref, hng_ref,
                 sel_ref, wo_ref, conv_ref, c_ref, m_ref, s_ref, att_ref, layer, side_tasks):
    t = CHUNK
    dh = HEAD_DIM
    gw = GROUP_WIDTH
    heads = range(HEADS)
    causal, tri_l, level_masks, diag_mask, upper = consts
    merged = [None] * (2 * HEADS)

    def head(x, h):
        return x[:, h * dh:(h + 1) * dh]

    def mlstm():
        gc = gc_ref[...] + gbc_ref[...]
        bc_c = sum(_dot(tri_l, part) for part in _split3(_log_sigmoid(gc)))
        gr = gc.T
        bc_r = bc_c.T
        yield
        conv_ref[CONV_HALO:CONV_HALO + t, :] = p_ref[:, 0:2 * gw]
        xc = cb_ref[...]
        for j in range(CONV_WIDTH):
            start = CONV_HALO - (CONV_WIDTH - 1) + j
            xc = xc + cw_ref[j:j + 1, :] * conv_ref[start:start + t, :]
        conv_ref[0:CONV_HALO, :] = conv_ref[t:t + CONV_HALO, :]
        qk = _silu(xc)
        q_b = [head(qk, h).astype(BF16) for h in heads]
        k_f = [head(qk, HEADS + h) * (dh ** -0.5) for h in heads]
        yield
        ones_b = jnp.ones((t, dh), BF16)
        v_aug = [jnp.concatenate([head(p_ref[:, 2 * gw:3 * gw], h).astype(BF16), ones_b], axis=1)
                 for h in heads]
        li_c = [gc[:, h:h + 1] for h in heads]
        b_c = [bc_c[:, HEADS + h:HEADS + h + 1] for h in heads]
        m_prev = [m_ref[h:h + 1, 0:1] for h in heads]
        d = [jnp.where(causal, b_c[h] - bc_r[HEADS + h:HEADS + h + 1, :] + gr[h:h + 1, :], -jnp.inf)
             for h in heads]
        inter = [b_c[h] + m_prev[h] for h in heads]
        m_t = [jnp.maximum(inter[h], jnp.max(d[h], axis=-1, keepdims=True)) for h in heads]
        s = [(_dot_nt(q_b[h], k_f[h].astype(BF16)) * jnp.exp(d[h] - m_t[h])).astype(BF16)
             for h in heads]
        yield
        c_aug = [c_ref[h] for h in heads]
        out = [jnp.exp(inter[h] - m_t[h]) * _dot(q_b[h], c_aug[h].astype(BF16)) + _dot(s[h], v_aug[h])
               for h in heads]
        hm = [out[h][:, 0:dh] / jnp.maximum(jnp.abs(out[h][:, dh:2 * dh]), jnp.exp(-m_t[h]))
              for h in heads]
        yield
        for h in heads:
            b_last = b_c[h][t - 1:t, :]
            m_new = m_t[h][t - 1:t, :]
            g = jnp.exp(b_last - b_c[h] + li_c[h] - m_new)
            decay = jnp.exp(b_last + m_prev[h] - m_new)
            c_ref[h] = decay * c_aug[h] + _dot_tn((g * k_f[h]).astype(BF16), v_aug[h])
            m_ref[h:h + 1, :] = jnp.broadcast_to(m_new, (1, V7X_LANES))
        yield
        gate_m = _sigmoid(p_ref[:, 3 * gw:4 * gw])
        for h in heads:
            merged[h] = (_rms(hm[h]) * head(mg_ref[...], h) * head(gate_m, h)).astype(BF16)

    def hgrn():
        lbl = lbl_ref[...]
        e_l = jnp.exp(lbl - jnp.max(lbl, axis=0, keepdims=True))
        lb = jnp.sum(e_l[0:layer + 1, :], axis=0, keepdims=True) / jnp.sum(e_l, axis=0, keepdims=True)
        fg = lb + (1.0 - lb) * _sigmoid(p_ref[:, 5 * gw:6 * gw])
        lf = jnp.log(fg)
        lf_hi = lf.astype(BF16)
        lf_lo = (lf - lf_hi.astype(F32)).astype(BF16)
        logdec = _dot(sel_ref[...], jnp.concatenate([lf_hi, lf_lo], axis=0))
        yield
        qh = _silu(p_ref[:, 4 * gw:5 * gw])
        kh = 1.0 - fg
        qh_b = qh.astype(BF16)
        kh_b = kh.astype(BF16)
        for h in heads:
            pltpu.store(att_ref.at[h], _dot_nt(head(qh_b, h), head(kh_b, h)), mask=diag_mask)
        for l in range(LEVELS):
            if l == 0:
                z = jnp.where(upper[0], qh * fg, kh).astype(BF16)
            else:
                e = jnp.exp(logdec[(l + 1) * t:(l + 2) * t, :])
                z = (jnp.where(upper[l], qh, kh) * e).astype(BF16)
            for h in heads:
                pltpu.store(att_ref.at[h], _dot_nt(head(z, h), head(z, h)), mask=level_masks[l])
            yield
        bc = logdec[0:t, :]
        v_b = p_ref[:, 6 * gw:7 * gw].astype(BF16)
        st = [s_ref[h] for h in heads]
        qe = (qh * jnp.exp(bc)).astype(BF16)
        o = [_dot_nt(head(qe, h), st[h].astype(BF16)) + _dot(att_ref[h].astype(BF16), head(v_b, h))
             for h in heads]
        yield
        kd = (kh * jnp.exp(logdec[t:2 * t, :])).astype(BF16)
        dec = jnp.exp(bc[t - 1:t, :])
        for h in heads:
            s_ref[h] = head(dec, h) * st[h] + _dot_tn(head(v_b, h), head(kd, h))
        yield
        gate_h = _silu(p_ref[:, 7 * gw:8 * gw])
        for h in heads:
            merged[HEADS + h] = (_rms(o[h]) * head(hng_ref[...], h) * head(gate_h, h)).astype(BF16)

    def side():
        for task in side_tasks:
            task()
            yield

    streams = [side(), hgrn(), mlstm()]
    while streams:
        for stream in list(streams):
            if next(stream, streams) is streams:
                streams.remove(stream)
    return _dot(jnp.concatenate(merged, axis=1), wo_ref[...])


def _mixer_kernel(h_ref, hn_ref, mod_ref, modn_ref, pre_ref, post_ref, wm_ref, wgc_ref, wo_ref,
                  cw_ref, cb_ref, gbc_ref, mg_ref, lbl_ref, hng_ref, sel_ref,
                  o_ref,
                  pa_ref, pb_ref, gca_ref, gcb_ref, yb_ref, conv_ref, c_ref, m_ref, s_ref, att_ref,
                  *, layer, steps_per_seq, row0):
    t = CHUNK
    pr = PROJ_ROWS
    i = pl.program_id(0)
    bufs = ((pa_ref, gca_ref), (pb_ref, gcb_ref))

    def project_tasks(x, md_ref, buf):
        return _project_tasks(x, md_ref, pre_ref, wm_ref, wgc_ref, yb_ref, *buf, row0)

    @pl.when(i == 0)
    def _():
        att_ref[...] = jnp.zeros(att_ref.shape, F32)
        for task in project_tasks(h_ref[0:pr, :], mod_ref, bufs[0]):
            task()

    @pl.when(i % steps_per_seq == 0)
    def _():
        conv_ref[0:CONV_HALO, :] = jnp.zeros((CONV_HALO, 2 * GROUP_WIDTH), F32)
        c_ref[...] = jnp.zeros(c_ref.shape, F32)
        m_ref[...] = jnp.full(m_ref.shape, M_NEG, F32)
        s_ref[...] = jnp.zeros(s_ref.shape, F32)

    gate_scale = mod_ref[row0 + 2:row0 + 3, :] * post_ref[...]
    consts = _chunk_constants()
    chunks_per_proj = pr // t
    for half in range(STEP_ROWS // pr):
        p_ref, gc_ref = bufs[half % 2]
        if (half + 1) * pr < STEP_ROWS:
            side = project_tasks(h_ref[(half + 1) * pr:(half + 2) * pr, :], mod_ref, bufs[(half + 1) % 2])
        else:
            side = project_tasks(hn_ref[...], modn_ref, bufs[(half + 1) % 2])
        per_chunk = len(side) // chunks_per_proj
        for k in range(chunks_per_proj):
            rows = slice(half * pr + k * t, half * pr + (k + 1) * t)
            y = _recur_chunk(p_ref.at[k * t:(k + 1) * t, :], gc_ref.at[k * t:(k + 1) * t, :],
                             consts, cw_ref, cb_ref, gbc_ref, mg_ref, lbl_ref, hng_ref,
                             sel_ref, wo_ref, conv_ref, c_ref, m_ref, s_ref, att_ref, layer,
                             side[k * per_chunk:(k + 1) * per_chunk])
            o_ref[rows, :] = h_ref[rows, :] + gate_scale * _rms(y)


def _mixer(h, mod, pre_g, post_g, wm, wgc, wo, conv_w, conv_b, gate_b_col,
           mnorm_g, lb_logits, hnorm_g, *, layer, row0, seq):
    n, d = h.shape
    tm = STEP_ROWS
    pr = PROJ_ROWS
    steps = n // tm
    steps_per_seq = seq // tm
    last_proj = n // pr - 1
    gw = GROUP_WIDTH
    resident = pl.Buffered(1)
    const = lambda i: (0, 0)
    nxt_proj = lambda i: jnp.minimum((tm // pr) * (i + 1), last_proj)
    small = [conv_w, conv_b, gate_b_col, mnorm_g, lb_logits, hnorm_g, _decay_selector()]
    est = (wm.size * 2 + wo.size * 2 + 2 * pr * PROJ_COLS * 4
           + 6 * tm * d * 4 + 16 * 1024 * 1024)
    return pl.pallas_call(
        functools.partial(_mixer_kernel, layer=layer, steps_per_seq=steps_per_seq, row0=row0),
        grid=(steps,),
        in_specs=[
            pl.BlockSpec((tm, d), lambda i: (i, 0)),
            pl.BlockSpec((pr, d), lambda i: (nxt_proj(i), 0)),
            pl.BlockSpec((None, N_MOD, d), lambda i: (i // steps_per_seq, 0, 0)),
            pl.BlockSpec((None, N_MOD, d), lambda i: (nxt_proj(i) // (seq // pr), 0, 0)),
            pl.BlockSpec((1, d), const),
            pl.BlockSpec((1, d), const),
            pl.BlockSpec(wm.shape, lambda i: (0, 0, 0), pipeline_mode=resident),
            pl.BlockSpec(wgc.shape, const),
            pl.BlockSpec(wo.shape, const, pipeline_mode=resident),
        ] + [pl.BlockSpec(a.shape, const) for a in small],
        out_specs=pl.BlockSpec((tm, d), lambda i: (i, 0)),
        out_shape=jax.ShapeDtypeStruct((n, d), F32),
        scratch_shapes=[
            pltpu.VMEM((pr, PROJ_COLS), F32),
            pltpu.VMEM((pr, PROJ_COLS), F32),
            pltpu.VMEM((pr, V7X_LANES), F32),
            pltpu.VMEM((pr, V7X_LANES), F32),
            pltpu.VMEM((pr, d), BF16),
            pltpu.VMEM((CHUNK + CONV_HALO, 2 * gw), F32),
            pltpu.VMEM((HEADS, HEAD_DIM, 2 * HEAD_DIM), F32),
            pltpu.VMEM((V7X_SUBLANES, V7X_LANES), F32),
            pltpu.VMEM((HEADS, HEAD_DIM, HEAD_DIM), F32),
            pltpu.VMEM((HEADS, CHUNK, CHUNK), F32),
        ],
        compiler_params=pltpu.CompilerParams(
            dimension_semantics=("arbitrary",), vmem_limit_bytes=_vmem_limit(est)),
        name="mixer",
    )(h, h, mod, mod, pre_g, post_g, wm, wgc, wo, *small)


def _ffn_weights(w_gate, w_up, w_down):
    d, f = w_gate.shape
    nf = f // FFN_CHUNK
    wd = w_down.reshape(nf, FFN_CHUNK, d).astype(BF16)
    return w_gate.astype(BF16), w_up.astype(BF16), wd


def kernel(x, c, ada_w, ada_b, ffn1_pre_g, ffn1_post_g, ffn1_w_gate, ffn1_w_up, ffn1_w_down, mix_pre_g, mix_post_g, w_in, mlstm_conv_w, mlstm_conv_b, mlstm_gate_b, mlstm_norm_g, hgrn_lb_logits, hgrn_norm_g, w_out, ffn2_pre_g, ffn2_post_g, ffn2_w_gate, ffn2_w_up, ffn2_w_down):
    batch, seq, d = x.shape
    n = batch * seq
    gw = GROUP_WIDTH
    depth = ada_w.shape[0]
    assert seq % FFN_TOKENS == 0 and seq % STEP_ROWS == 0 and d % V7X_LANES == 0
    assert w_in.shape[2] == PROJ_COLS + 2 * HEADS

    h = x.reshape(n, d)
    for l in range(depth):
        mod = _adaln(c, ada_w[l], ada_b[l]).reshape(batch, N_MOD, d)

        h = _ffn(h, mod, ffn1_pre_g[l:l + 1], ffn1_post_g[l:l + 1],
                 *_ffn_weights(ffn1_w_gate[l], ffn1_w_up[l], ffn1_w_down[l]), row0=0, seq=seq)

        w = w_in[l]
        w_main = jnp.concatenate([w[:, :4 * gw], w[:, 4 * gw + 2 * HEADS:]], axis=1).astype(BF16)
        w_main = w_main.reshape(d, PROJ_COLS // gw, gw).transpose(1, 0, 2)
        w_gates = w[:, 4 * gw:4 * gw + 2 * HEADS]
        w_gc = jnp.pad(w_gates, ((0, 0), (0, V7X_LANES - 2 * HEADS))).astype(BF16)
        gate_b_col = jnp.pad(mlstm_gate_b[l], (0, V7X_LANES - 2 * HEADS)).reshape(1, V7X_LANES)

        h = _mixer(h, mod, mix_pre_g[l:l + 1], mix_post_g[l:l + 1], w_main, w_gc,
                   w_out[l].astype(BF16), mlstm_conv_w[l], mlstm_conv_b[l:l + 1], gate_b_col,
                   mlstm_norm_g[l:l + 1], hgrn_lb_logits, hgrn_norm_g[l:l + 1],
                   layer=l, row0=3, seq=seq)

        h = _ffn(h, mod, ffn2_pre_g[l:l + 1], ffn2_post_g[l:l + 1],
                 *_ffn_weights(ffn2_w_gate[l], ffn2_w_up[l], ffn2_w_down[l]), row0=6, seq=seq)
    return h.reshape(batch, seq, d)
```

```python
import functools

import numpy as np
import jax
import jax.numpy as jnp
from jax import lax
from jax.experimental import pallas as pl
from jax.experimental.pallas import tpu as pltpu

F32 = jnp.float32
BF16 = jnp.bfloat16

V7X_LANES = 128
V7X_SUBLANES = 8
V7X_MXU_WIDTH = 256
V7X_VMEM_BYTES = 64 * 1024 * 1024

EPS = 1e-6
M_NEG = -1e30
N_MOD = 9
HEADS = 4
HEAD_DIM = 128
GROUP_WIDTH = HEADS * HEAD_DIM
CONV_WIDTH = 4
FFN_RESIDUAL_WEIGHT = 0.5

FFN_TOKENS = 1024
FFN_PIECES = 4
FFN_CHUNK = V7X_MXU_WIDTH
CHUNK = 128
PROJ_ROWS = 2 * CHUNK
STEP_ROWS = 2 * PROJ_ROWS
LEVELS = 7
PROJ_COLS = 8 * GROUP_WIDTH
CONV_HALO = V7X_SUBLANES
VEC_PRE, VEC_POST, VEC_CONV_B, VEC_HEAD_NORM, VEC_GATE_B, VEC_CONV_W = (r * V7X_SUBLANES for r in range(6))
VEC_ROWS = 8 * V7X_SUBLANES
M_ROWS = 4 * V7X_SUBLANES


def _sigmoid(x):
    return 1.0 / (1.0 + jnp.exp(-x))


def _silu(x):
    return x * _sigmoid(x)


def _log_sigmoid(x):
    return jnp.minimum(x, 0.0) - jnp.log(1.0 + jnp.exp(-jnp.abs(x)))


def _rms(x):
    return x * lax.rsqrt(jnp.mean(x * x, axis=-1, keepdims=True) + EPS)


def _dot(a, b):
    return jnp.dot(a, b, preferred_element_type=F32)


def _dot_nt(a, b):
    return lax.dot_general(a, b, (((1,), (1,)), ((), ())), preferred_element_type=F32)


def _dot_tn(a, b):
    return lax.dot_general(a, b, (((0,), (0,)), ((), ())), preferred_element_type=F32)


def _split3(x):
    p1 = x.astype(BF16)
    r1 = x - p1.astype(F32)
    p2 = r1.astype(BF16)
    p3 = (r1 - p2.astype(F32)).astype(BF16)
    return p1, p2, p3


def _vmem_limit(nbytes):
    return int(min(nbytes, V7X_VMEM_BYTES - 4 * 1024 * 1024))


def _mod_kernel(c_ref, w_ref, b_ref, o_ref):
    sc = _silu(c_ref[...]).astype(BF16)
    o_ref[...] = _dot(sc, w_ref[...].astype(BF16)) + b_ref[...]


def _adaln(c, w, b):
    bsz, d = c.shape
    n = w.shape[1]
    return pl.pallas_call(
        _mod_kernel,
        grid=(n // d,),
        in_specs=[
            pl.BlockSpec((bsz, d), lambda j: (0, 0)),
            pl.BlockSpec((d, d), lambda j: (0, j)),
            pl.BlockSpec((1, d), lambda j: (0, j)),
        ],
        out_specs=pl.BlockSpec((bsz, d), lambda j: (0, j)),
        out_shape=jax.ShapeDtypeStruct((bsz, n), F32),
        name="adaln_mod",
    )(c, w, b.reshape(1, n))


def _ffn_kernel(x_ref, mod_ref, pre_ref, post_ref, wg_ref, wu_ref, wd_ref, o_ref, yb_ref, acc_ref, *, row0):
    tm = x_ref.shape[0]
    half = tm // 2
    piece = half // FFN_PIECES
    in_scale = pre_ref[...] * (1.0 + mod_ref[row0 + 1:row0 + 2, :])
    in_shift = mod_ref[row0:row0 + 1, :]
    out_scale = (FFN_RESIDUAL_WEIGHT * mod_ref[row0 + 2:row0 + 3, :]) * post_ref[...]

    def prologue(r):
        rows = slice(r, r + piece)
        yb_ref[rows, :] = (_rms(x_ref[rows, :]) * in_scale + in_shift).astype(BF16)

    def epilogue(r):
        rows = slice(r, r + piece)
        o_ref[rows, :] = x_ref[rows, :] + out_scale * _rms(acc_ref[rows, :])

    def swiglu(r, side_tasks):
        side_tasks = list(side_tasks)
        yb = yb_ref[r:r + half, :]
        acc = None
        for c in range(wd_ref.shape[0]):
            cols = slice(c * FFN_CHUNK, (c + 1) * FFN_CHUNK)
            g = _dot(yb, wg_ref[:, cols])
            u = _dot(yb, wu_ref[:, cols])
            part = _dot((_silu(g) * u).astype(BF16), wd_ref[c])
            acc = part if acc is None else acc + part
            if side_tasks:
                side_tasks.pop(0)()
        acc_ref[r:r + half, :] = acc

    pieces = [k * piece for k in range(FFN_PIECES)]
    for r in pieces:
        prologue(r)
    swiglu(0, [functools.partial(prologue, half + r) for r in pieces])
    swiglu(half, [functools.partial(epilogue, r) for r in pieces])
    for r in pieces:
        epilogue(half + r)


def _ffn(h, mod, pre_g, post_g, wg, wu, wd, *, row0, seq):
    n, d = h.shape
    tm = FFN_TOKENS
    tiles_per_seq = seq // tm
    nf, fc, _ = wd.shape
    const3 = lambda i: (0, 0, 0)
    resident = pl.Buffered(1)
    est = (4 * tm * d * 4
           + 3 * nf * d * fc * 2
           + tm * d * (2 + 4)
           + 3 * tm * d * 4)
    return pl.pallas_call(
        functools.partial(_ffn_kernel, row0=row0),
        grid=(n // tm,),
        in_specs=[
            pl.BlockSpec((tm, d), lambda i: (i, 0)),
            pl.BlockSpec((None, N_MOD, d), lambda i: (i // tiles_per_seq, 0, 0)),
            pl.BlockSpec((1, d), lambda i: (0, 0)),
            pl.BlockSpec((1, d), lambda i: (0, 0)),
            pl.BlockSpec(wg.shape, lambda i: (0, 0), pipeline_mode=resident),
            pl.BlockSpec(wu.shape, lambda i: (0, 0), pipeline_mode=resident),
            pl.BlockSpec((nf, fc, d), const3, pipeline_mode=resident),
        ],
        out_specs=pl.BlockSpec((tm, d), lambda i: (i, 0)),
        out_shape=jax.ShapeDtypeStruct((n, d), F32),
        scratch_shapes=[pltpu.VMEM((tm, d), BF16),
                        pltpu.VMEM((tm, d), F32)],
        compiler_params=pltpu.CompilerParams(
            dimension_semantics=("arbitrary",), vmem_limit_bytes=_vmem_limit(est)),
        name="ffn",
    )(h, mod, pre_g, post_g, wg, wu, wd)


def _project_tasks(x, mod_ref, pre_ref, wm_ref, wgc_ref, yb_ref, p_ref, gc_ref, row0):
    w = GROUP_WIDTH

    def piece(c):
        def run():
            if c == 0:
                y = _rms(x) * pre_ref[...]
                y = y * (1.0 + mod_ref[row0 + 1:row0 + 2, :]) + mod_ref[row0:row0 + 1, :]
                yb_ref[...] = y.astype(BF16)
            yb = yb_ref[...]
            p_ref[:, c * w:(c + 1) * w] = _dot(yb, wm_ref[c])
            if c == PROJ_COLS // w - 1:
                gc_ref[...] = _dot(yb, wgc_ref[...])
        return run

    return [piece(c) for c in range(PROJ_COLS // w)]


def _decay_selector():
    t = CHUNK
    tt, rr = np.meshgrid(np.arange(t), np.arange(t), indexing="ij")
    blocks = [rr <= tt, rr > tt]
    for l in range(1, LEVELS):
        half = 1 << l
        anchor = (tt & ~(2 * half - 1)) + half - 1
        is_upper = ((tt >> l) & 1) == 1
        blocks.append(np.where(is_upper, (rr > anchor) & (rr <= tt), (rr > tt) & (rr <= anchor)))
    sel = np.concatenate(blocks, axis=0).astype(np.float32)
    return jnp.asarray(np.concatenate([sel, sel], axis=1), dtype=BF16)


def _chunk_constants():
    t = CHUNK
    row = lax.broadcasted_iota(jnp.int32, (t, t), 0)
    col = lax.broadcasted_iota(jnp.int32, (t, t), 1)
    causal = col <= row
    tri_l = jnp.where(causal, 1.0, 0.0).astype(BF16)
    xor = row ^ col
    level_masks = [(col < row) & ((xor >> l) == 1) for l in range(LEVELS)]
    diag_mask = row == col
    t_col = lax.broadcasted_iota(jnp.int32, (t, 1), 0)
    upper = [((t_col >> l) & 1) == 1 for l in range(LEVELS)]
    return causal, tri_l, level_masks, diag_mask, upper


def _recur_chunk(p_ref, gc_ref, consts, cw_ref, cb_ref, gbc_ref, mg_ref, lbl_ref, hng_ref,
                 sel_ref, wo_ref, conv_ref, c_ref, m_ref, s_ref, att_ref, layer, side_tasks):
    t = CHUNK
    dh = HEAD_DIM
    gw = GROUP_WIDTH
    heads = range(HEADS)
    causal, tri_l, level_masks, diag_mask, upper = consts
    merged = [None] * (2 * HEADS)

    def head(x, h):
        return x[:, h * dh:(h + 1) * dh]

    def mlstm():
        gc = gc_ref[...] + gbc_ref[...]
        bc_c = sum(_dot(tri_l, part) for part in _split3(_log_sigmoid(gc)))
        gr = gc.T
        bc_r = bc_c.T
        yield
        conv_ref[CONV_HALO:CONV_HALO + t, :] = p_ref[:, 0:2 * gw]
        xc = cb_ref[...]
        for j in range(CONV_WIDTH):
            start = CONV_HALO - (CONV_WIDTH - 1) + j
            xc = xc + cw_ref[j:j + 1, :] * conv_ref[start:start + t, :]
        conv_ref[0:CONV_HALO, :] = conv_ref[t:t + CONV_HALO, :]
        qk = _silu(xc)
        q_b = [head(qk, h).astype(BF16) for h in heads]
        k_f = [head(qk, HEADS + h) * (dh ** -0.5) for h in heads]
        yield
        ones_b = jnp.ones((t, dh), BF16)
        v_aug = [jnp.concatenate([head(p_ref[:, 2 * gw:3 * gw], h).astype(BF16), ones_b], axis=1)
                 for h in heads]
        li_c = [gc[:, h:h + 1] for h in heads]
        b_c = [bc_c[:, HEADS + h:HEADS + h + 1] for h in heads]
        m_prev = [m_ref[h:h + 1, 0:1] for h in heads]
        d = [jnp.where(causal, b_c[h] - bc_r[HEADS + h:HEADS + h + 1, :] + gr[h:h + 1, :], -jnp.inf)
             for h in heads]
        inter = [b_c[h] + m_prev[h] for h in heads]
        m_t = [jnp.maximum(inter[h], jnp.max(d[h], axis=-1, keepdims=True)) for h in heads]
        s = [(_dot_nt(q_b[h], k_f[h].astype(BF16)) * jnp.exp(d[h] - m_t[h])).astype(BF16)
             for h in heads]
        yield
        c_aug = [c_ref[h] for h in heads]
        out = [jnp.exp(inter[h] - m_t[h]) * _dot(q_b[h], c_aug[h].astype(BF16)) + _dot(s[h], v_aug[h])
               for h in heads]
        hm = [out[h][:, 0:dh] / jnp.maximum(jnp.abs(out[h][:, dh:2 * dh]), jnp.exp(-m_t[h]))
              for h in heads]
        yield
        for h in heads:
            b_last = b_c[h][t - 1:t, :]
            m_new = m_t[h][t - 1:t, :]
            g = jnp.exp(b_last - b_c[h] + li_c[h] - m_new)
            decay = jnp.exp(b_last + m_prev[h] - m_new)
            c_ref[h] = decay * c_aug[h] + _dot_tn((g * k_f[h]).astype(BF16), v_aug[h])
            m_ref[h:h + 1, :] = jnp.broadcast_to(m_new, (1, V7X_LANES))
        yield
        gate_m = _sigmoid(p_ref[:, 3 * gw:4 * gw])
        for h in heads:
            merged[h] = (_rms(hm[h]) * head(mg_ref[...], h) * head(gate_m, h)).astype(BF16)

    def hgrn():
        lbl = lbl_ref[...]
        e_l = jnp.exp(lbl - jnp.max(lbl, axis=0, keepdims=True))
        lb = jnp.sum(e_l[0:layer + 1, :], axis=0, keepdims=True) / jnp.sum(e_l, axis=0, keepdims=True)
        fg = lb + (1.0 - lb) * _sigmoid(p_ref[:, 5 * gw:6 * gw])
        lf = jnp.log(fg)
        lf_hi = lf.astype(BF16)
        lf_lo = (lf - lf_hi.astype(F32)).astype(BF16)
        logdec = _dot(sel_ref[...], jnp.concatenate([lf_hi, lf_lo], axis=0))
        yield
        qh = _silu(p_ref[:, 4 * gw:5 * gw])
        kh = 1.0 - fg
        qh_b = qh.astype(BF16)
        kh_b = kh.astype(BF16)
        for h in heads:
            pltpu.store(att_ref.at[h], _dot_nt(head(qh_b, h), head(kh_b, h)), mask=diag_mask)
        for l in range(LEVELS):
            if l == 0:
                z = jnp.where(upper[0], (qh * fg).astype(BF16), kh_b)
            else:
                e = jnp.exp(logdec[(l + 1) * t:(l + 2) * t, :]).astype(BF16)
                z = jnp.where(upper[l], qh_b, kh_b) * e
            for h in heads:
                pltpu.store(att_ref.at[h], _dot_nt(head(z, h), head(z, h)), mask=level_masks[l])
            yield
        bc = logdec[0:t, :]
        v_b = p_ref[:, 6 * gw:7 * gw].astype(BF16)
        st = [s_ref[h] for h in heads]
        qe = (qh * jnp.exp(bc)).astype(BF16)
        o = [_dot_nt(head(qe, h), st[h].astype(BF16)) + _dot(att_ref[h].astype(BF16), head(v_b, h))
             for h in heads]
        yield
        kd = (kh * jnp.exp(logdec[t:2 * t, :])).astype(BF16)
        dec = jnp.exp(bc[t - 1:t, :])
        for h in heads:
            s_ref[h] = head(dec, h) * st[h] + _dot_tn(head(v_b, h), head(kd, h))
        yield
        gate_h = _silu(p_ref[:, 7 * gw:8 * gw])
        for h in heads:
            merged[HEADS + h] = (_rms(o[h]) * head(hng_ref[...], h) * head(gate_h, h)).astype(BF16)

    def side():
        for task in side_tasks:
            task()
            yield

    streams = [side(), hgrn(), mlstm()]
    while streams:
        for stream in list(streams):
            if next(stream, streams) is streams:
                streams.remove(stream)
    return _dot(jnp.concatenate(merged, axis=1), wo_ref[...])


def _mixer_kernel(h_ref, hn_ref, mod_ref, modn_ref, vec_ref, wm_ref, wgc_ref, wo_ref, lbl_ref, sel_ref,
                  o_ref,
                  pa_ref, pb_ref, gca_ref, gcb_ref, yb_ref, conv_ref, c_ref, m_ref, s_ref, att_ref,
                  *, layer, steps_per_seq, row0):
    t = CHUNK
    pr = PROJ_ROWS
    i = pl.program_id(0)
    bufs = ((pa_ref, gca_ref), (pb_ref, gcb_ref))
    gw = GROUP_WIDTH
    pre_ref, post_ref, cb_ref = (vec_ref.at[r:r + 1, :] for r in (VEC_PRE, VEC_POST, VEC_CONV_B))
    mg_ref = vec_ref.at[VEC_HEAD_NORM:VEC_HEAD_NORM + 1, 0:gw]
    hng_ref = vec_ref.at[VEC_HEAD_NORM:VEC_HEAD_NORM + 1, gw:2 * gw]
    gbc_ref = vec_ref.at[VEC_GATE_B:VEC_GATE_B + 1, 0:V7X_LANES]
    cw_ref = vec_ref.at[VEC_CONV_W:VEC_CONV_W + CONV_WIDTH, :]

    def project_tasks(x, md_ref, buf):
        return _project_tasks(x, md_ref, pre_ref, wm_ref, wgc_ref, yb_ref, *buf, row0)

    @pl.when(i == 0)
    def _():
        att_ref[...] = jnp.zeros(att_ref.shape, F32)
        for task in project_tasks(h_ref[0:pr, :], mod_ref, bufs[0]):
            task()

    @pl.when(i % steps_per_seq == 0)
    def _():
        conv_ref[0:CONV_HALO, :] = jnp.zeros((CONV_HALO, 2 * GROUP_WIDTH), F32)
        c_ref[...] = jnp.zeros(c_ref.shape, F32)
        m_ref[...] = jnp.full(m_ref.shape, M_NEG, F32)
        s_ref[...] = jnp.zeros(s_ref.shape, F32)

    gate_scale = mod_ref[row0 + 2:row0 + 3, :] * post_ref[...]
    consts = _chunk_constants()
    chunks_per_proj = pr // t
    for half in range(STEP_ROWS // pr):
        p_ref, gc_ref = bufs[half % 2]
        if (half + 1) * pr < STEP_ROWS:
            side = project_tasks(h_ref[(half + 1) * pr:(half + 2) * pr, :], mod_ref, bufs[(half + 1) % 2])
        else:
            side = project_tasks(hn_ref[...], modn_ref, bufs[(half + 1) % 2])
        per_chunk = len(side) // chunks_per_proj
        for k in range(chunks_per_proj):
            rows = slice(half * pr + k * t, half * pr + (k + 1) * t)
            y = _recur_chunk(p_ref.at[k * t:(k + 1) * t, :], gc_ref.at[k * t:(k + 1) * t, :],
                             consts, cw_ref, cb_ref, gbc_ref, mg_ref, lbl_ref, hng_ref,
                             sel_ref, wo_ref, conv_ref, c_ref, m_ref, s_ref, att_ref, layer,
                             side[k * per_chunk:(k + 1) * per_chunk])
            o_ref[rows, :] = h_ref[rows, :] + gate_scale * _rms(y)


def _mixer(h, mod, pre_g, post_g, wm, wgc, wo, conv_w, conv_b, gate_b,
           mnorm_g, lb_logits, hnorm_g, *, layer, row0, seq):
    n, d = h.shape
    tm = STEP_ROWS
    pr = PROJ_ROWS
    steps = n // tm
    steps_per_seq = seq // tm
    last_proj = n // pr - 1
    gw = GROUP_WIDTH
    resident = pl.Buffered(1)
    const = lambda i: (0, 0)
    nxt_proj = lambda i: jnp.minimum((tm // pr) * (i + 1), last_proj)
    rows = {VEC_PRE: pre_g[0], VEC_POST: post_g[0], VEC_CONV_B: conv_b[0],
            VEC_HEAD_NORM: jnp.concatenate([mnorm_g[0], hnorm_g[0]]),
            VEC_GATE_B: jnp.pad(gate_b, (0, d - gate_b.shape[0]))}
    rows.update({VEC_CONV_W + j: conv_w[j] for j in range(CONV_WIDTH)})
    vecs = jnp.stack([rows.get(r, jnp.zeros((d,), F32)) for r in range(VEC_ROWS)])
    small = [lb_logits, _decay_selector()]
    est = (wm.size * 2 + wo.size * 2 + 2 * pr * PROJ_COLS * 4
           + 6 * tm * d * 4 + 16 * 1024 * 1024)
    return pl.pallas_call(
        functools.partial(_mixer_kernel, layer=layer, steps_per_seq=steps_per_seq, row0=row0),
        grid=(steps,),
        in_specs=[
            pl.BlockSpec((tm, d), lambda i: (i, 0)),
            pl.BlockSpec((pr, d), lambda i: (nxt_proj(i), 0)),
            pl.BlockSpec((None, N_MOD, d), lambda i: (i // steps_per_seq, 0, 0)),
            pl.BlockSpec((None, N_MOD, d), lambda i: (nxt_proj(i) // (seq // pr), 0, 0)),
            pl.BlockSpec(vecs.shape, const),
            pl.BlockSpec(wm.shape, lambda i: (0, 0, 0), pipeline_mode=resident),
            pl.BlockSpec(wgc.shape, const),
            pl.BlockSpec(wo.shape, const, pipeline_mode=resident),
        ] + [pl.BlockSpec(a.shape, const) for a in small],
        out_specs=pl.BlockSpec((tm, d), lambda i: (i, 0)),
        out_shape=jax.ShapeDtypeStruct((n, d), F32),
        scratch_shapes=[
            pltpu.VMEM((pr, PROJ_COLS), F32),
            pltpu.VMEM((pr, PROJ_COLS), F32),
            pltpu.VMEM((pr, V7X_LANES), F32),
            pltpu.VMEM((pr, V7X_LANES), F32),
            pltpu.VMEM((pr, d), BF16),
            pltpu.VMEM((CHUNK + CONV_HALO, 2 * gw), F32),
            pltpu.VMEM((HEADS, HEAD_DIM, 2 * HEAD_DIM), F32),
            pltpu.VMEM((M_ROWS, V7X_LANES), F32),
            pltpu.VMEM((HEADS, HEAD_DIM, HEAD_DIM), F32),
            pltpu.VMEM((HEADS, CHUNK, CHUNK), F32),
        ],
        compiler_params=pltpu.CompilerParams(
            dimension_semantics=("arbitrary",), vmem_limit_bytes=_vmem_limit(est)),
        name="mixer",
    )(h, h, mod, mod, vecs, wm, wgc, wo, *small)


def _ffn_weights(w_gate, w_up, w_down):
    d, f = w_gate.shape
    nf = f // FFN_CHUNK
    wd = w_down.reshape(nf, FFN_CHUNK, d).astype(BF16)
    return w_gate.astype(BF16), w_up.astype(BF16), wd


def kernel(x, c, ada_w, ada_b, ffn1_pre_g, ffn1_post_g, ffn1_w_gate, ffn1_w_up, ffn1_w_down, mix_pre_g, mix_post_g, w_in, mlstm_conv_w, mlstm_conv_b, mlstm_gate_b, mlstm_norm_g, hgrn_lb_logits, hgrn_norm_g, w_out, ffn2_pre_g, ffn2_post_g, ffn2_w_gate, ffn2_w_up, ffn2_w_down):
    batch, seq, d = x.shape
    n = batch * seq
    gw = GROUP_WIDTH
    depth = ada_w.shape[0]
    assert seq % FFN_TOKENS == 0 and seq % STEP_ROWS == 0 and d % V7X_LANES == 0
    assert w_in.shape[2] == PROJ_COLS + 2 * HEADS

    h = x.reshape(n, d)
    for l in range(depth):
        mod = _adaln(c, ada_w[l], ada_b[l]).reshape(batch, N_MOD, d)

        h = _ffn(h, mod, ffn1_pre_g[l:l + 1], ffn1_post_g[l:l + 1],
                 *_ffn_weights(ffn1_w_gate[l], ffn1_w_up[l], ffn1_w_down[l]), row0=0, seq=seq)

        w = w_in[l]
        w_main = jnp.concatenate([w[:, :4 * gw], w[:, 4 * gw + 2 * HEADS:]], axis=1).astype(BF16)
        w_main = w_main.reshape(d, PROJ_COLS // gw, gw).transpose(1, 0, 2)
        w_gates = w[:, 4 * gw:4 * gw + 2 * HEADS]
        w_gc = jnp.pad(w_gates, ((0, 0), (0, V7X_LANES - 2 * HEADS))).astype(BF16)
        h = _mixer(h, mod, mix_pre_g[l:l + 1], mix_post_g[l:l + 1], w_main, w_gc,
                   w_out[l].astype(BF16), mlstm_conv_w[l], mlstm_conv_b[l:l + 1], mlstm_gate_b[l],
                   mlstm_norm_g[l:l + 1], hgrn_lb_logits, hgrn_norm_g[l:l + 1],
                   layer=l, row0=3, seq=seq)

        h = _ffn(h, mod, ffn2_pre_g[l:l + 1], ffn2_post_g[l:l + 1],
                 *_ffn_weights(ffn2_w_gate[l], ffn2_w_up[l], ffn2_w_down[l]), row0=6, seq=seq)
    return h.reshape(batch, seq, d)
```

```python
import functools

import numpy as np
import jax
import jax.numpy as jnp
from jax import lax
from jax.experimental import pallas as pl
from jax.experimental.pallas import tpu as pltpu

F32 = jnp.float32
BF16 = jnp.bfloat16

V7X_LANES = 128
V7X_SUBLANES = 8
V7X_MXU_WIDTH = 256
V7X_VMEM_BYTES = 64 * 1024 * 1024

EPS = 1e-6
M_NEG = -1e30
N_MOD = 9
HEADS = 4
HEAD_DIM = 128
GROUP_WIDTH = HEADS * HEAD_DIM
CONV_WIDTH = 4
FFN_RESIDUAL_WEIGHT = 0.5

FFN_TOKENS = 1024
FFN_PIECES = 4
FFN_CHUNK = V7X_MXU_WIDTH
CHUNK = 128
PROJ_ROWS = 2 * CHUNK
STEP_ROWS = 2 * PROJ_ROWS
LEVELS = 7
PROJ_COLS = 8 * GROUP_WIDTH
CONV_HALO = V7X_SUBLANES
VEC_PRE, VEC_POST, VEC_CONV_B, VEC_HEAD_NORM, VEC_GATE_B, VEC_CONV_W = (r * V7X_SUBLANES for r in range(6))
VEC_ROWS = 8 * V7X_SUBLANES
M_ROWS = 4 * V7X_SUBLANES


def _sigmoid(x):
    return 1.0 / (1.0 + jnp.exp(-x))


def _silu(x):
    return x * _sigmoid(x)


def _log_sigmoid(x):
    return jnp.minimum(x, 0.0) - jnp.log(1.0 + jnp.exp(-jnp.abs(x)))


def _rms(x):
    return x * lax.rsqrt(jnp.mean(x * x, axis=-1, keepdims=True) + EPS)


def _dot(a, b):
    return jnp.dot(a, b, preferred_element_type=F32)


def _dot_nt(a, b):
    return lax.dot_general(a, b, (((1,), (1,)), ((), ())), preferred_element_type=F32)


def _dot_tn(a, b):
    return lax.dot_general(a, b, (((0,), (0,)), ((), ())), preferred_element_type=F32)


def _split3(x):
    p1 = x.astype(BF16)
    r1 = x - p1.astype(F32)
    p2 = r1.astype(BF16)
    p3 = (r1 - p2.astype(F32)).astype(BF16)
    return p1, p2, p3


def _vmem_limit(nbytes):
    return int(min(nbytes, V7X_VMEM_BYTES - 4 * 1024 * 1024))


def _mod_kernel(c_ref, w_ref, b_ref, o_ref):
    sc = _silu(c_ref[...]).astype(BF16)
    o_ref[...] = _dot(sc, w_ref[...].astype(BF16)) + b_ref[...]


def _adaln(c, w, b):
    bsz, d = c.shape
    n = w.shape[1]
    return pl.pallas_call(
        _mod_kernel,
        grid=(n // d,),
        in_specs=[
            pl.BlockSpec((bsz, d), lambda j: (0, 0)),
            pl.BlockSpec((d, d), lambda j: (0, j)),
            pl.BlockSpec((1, d), lambda j: (0, j)),
        ],
        out_specs=pl.BlockSpec((bsz, d), lambda j: (0, j)),
        out_shape=jax.ShapeDtypeStruct((bsz, n), F32),
        name="adaln_mod",
    )(c, w, b.reshape(1, n))


def _ffn_kernel(x_ref, mod_ref, pre_ref, post_ref, wg_ref, wu_ref, wd_ref, o_ref, yb_ref, acc_ref, *, row0):
    tm = x_ref.shape[0]
    half = tm // 2
    piece = half // FFN_PIECES
    in_scale = pre_ref[...] * (1.0 + mod_ref[row0 + 1:row0 + 2, :])
    in_shift = mod_ref[row0:row0 + 1, :]
    out_scale = (FFN_RESIDUAL_WEIGHT * mod_ref[row0 + 2:row0 + 3, :]) * post_ref[...]

    def prologue(r):
        rows = slice(r, r + piece)
        yb_ref[rows, :] = (_rms(x_ref[rows, :]) * in_scale + in_shift).astype(BF16)

    def epilogue(r):
        rows = slice(r, r + piece)
        o_ref[rows, :] = x_ref[rows, :] + out_scale * _rms(acc_ref[rows, :])

    def swiglu(r, side_tasks):
        side_tasks = list(side_tasks)
        yb = yb_ref[r:r + half, :]
        acc = None
        for c in range(wd_ref.shape[0]):
            cols = slice(c * FFN_CHUNK, (c + 1) * FFN_CHUNK)
            g = _dot(yb, wg_ref[:, cols])
            u = _dot(yb, wu_ref[:, cols])
            part = _dot((_silu(g) * u).astype(BF16), wd_ref[c])
            acc = part if acc is None else acc + part
            if side_tasks:
                side_tasks.pop(0)()
        acc_ref[r:r + half, :] = acc

    pieces = [k * piece for k in range(FFN_PIECES)]
    for r in pieces:
        prologue(r)
    swiglu(0, [functools.partial(prologue, half + r) for r in pieces])
    swiglu(half, [functools.partial(epilogue, r) for r in pieces])
    for r in pieces:
        epilogue(half + r)


def _ffn(h, mod, pre_g, post_g, wg, wu, wd, *, row0, seq):
    n, d = h.shape
    tm = FFN_TOKENS
    tiles_per_seq = seq // tm
    nf, fc, _ = wd.shape
    const3 = lambda i: (0, 0, 0)
    resident = pl.Buffered(1)
    est = (4 * tm * d * 4
           + 3 * nf * d * fc * 2
           + tm * d * (2 + 4)
           + 3 * tm * d * 4)
    return pl.pallas_call(
        functools.partial(_ffn_kernel, row0=row0),
        grid=(n // tm,),
        in_specs=[
            pl.BlockSpec((tm, d), lambda i: (i, 0)),
            pl.BlockSpec((None, N_MOD, d), lambda i: (i // tiles_per_seq, 0, 0)),
            pl.BlockSpec((1, d), lambda i: (0, 0)),
            pl.BlockSpec((1, d), lambda i: (0, 0)),
            pl.BlockSpec(wg.shape, lambda i: (0, 0), pipeline_mode=resident),
            pl.BlockSpec(wu.shape, lambda i: (0, 0), pipeline_mode=resident),
            pl.BlockSpec((nf, fc, d), const3, pipeline_mode=resident),
        ],
        out_specs=pl.BlockSpec((tm, d), lambda i: (i, 0)),
        out_shape=jax.ShapeDtypeStruct((n, d), F32),
        scratch_shapes=[pltpu.VMEM((tm, d), BF16),
                        pltpu.VMEM((tm, d), F32)],
        compiler_params=pltpu.CompilerParams(
            dimension_semantics=("arbitrary",), vmem_limit_bytes=_vmem_limit(est)),
        name="ffn",
    )(h, mod, pre_g, post_g, wg, wu, wd)


def _project_tasks(x, mod_ref, pre_ref, wm_ref, wgc_ref, yb_ref, p_ref, gc_ref, row0):
    w = GROUP_WIDTH

    def piece(c):
        def run():
            if c == 0:
                y = _rms(x) * pre_ref[...]
                y = y * (1.0 + mod_ref[row0 + 1:row0 + 2, :]) + mod_ref[row0:row0 + 1, :]
                yb_ref[...] = y.astype(BF16)
            yb = yb_ref[...]
            p_ref[:, c * w:(c + 1) * w] = _dot(yb, wm_ref[c])
            if c == PROJ_COLS // w - 1:
                gc_ref[...] = _dot(yb, wgc_ref[...])
        return run

    return [piece(c) for c in range(PROJ_COLS // w)]


def _decay_selector():
    t = CHUNK
    tt, rr = np.meshgrid(np.arange(t), np.arange(t), indexing="ij")
    blocks = [rr <= tt, rr > tt]
    for l in range(1, LEVELS):
        half = 1 << l
        anchor = (tt & ~(2 * half - 1)) + half - 1
        is_upper = ((tt >> l) & 1) == 1
        blocks.append(np.where(is_upper, (rr > anchor) & (rr <= tt), (rr > tt) & (rr <= anchor)))
    sel = np.concatenate(blocks, axis=0).astype(np.float32)
    return jnp.asarray(np.concatenate([sel, sel], axis=1), dtype=BF16)


def _chunk_constants():
    t = CHUNK
    row = lax.broadcasted_iota(jnp.int32, (t, t), 0)
    col = lax.broadcasted_iota(jnp.int32, (t, t), 1)
    causal = col <= row
    tri_l = jnp.where(causal, 1.0, 0.0).astype(BF16)
    xor = row ^ col
    level_masks = [(col < row) & ((xor >> l) == 1) for l in range(LEVELS)]
    diag_mask = row == col
    t_col = lax.broadcasted_iota(jnp.int32, (t, 1), 0)
    upper = [((t_col >> l) & 1) == 1 for l in range(LEVELS)]
    return causal, tri_l, level_masks, diag_mask, upper


def _recur_chunk(p_ref, gc_ref, consts, cw_ref, cb_ref, gbc_ref, mg_ref, lbl_ref, hng_ref,
                 sel_ref, wo_ref, conv_ref, c_ref, m_ref, s_ref, att_ref, layer, side_tasks):
    t = CHUNK
    dh = HEAD_DIM
    gw = GROUP_WIDTH
    heads = range(HEADS)
    causal, tri_l, level_masks, diag_mask, upper = consts
    merged = [None] * (2 * HEADS)
    parts_done = object()

    def head(x, h):
        return x[:, h * dh:(h + 1) * dh]

    def mlstm():
        gc = gc_ref[...] + gbc_ref[...]
        bc_c = sum(_dot(tri_l, part) for part in _split3(_log_sigmoid(gc)))
        gr = gc.T
        bc_r = bc_c.T
        yield
        conv_ref[CONV_HALO:CONV_HALO + t, :] = p_ref[:, 0:2 * gw]
        xc = cb_ref[...]
        for j in range(CONV_WIDTH):
            start = CONV_HALO - (CONV_WIDTH - 1) + j
            xc = xc + cw_ref[j:j + 1, :] * conv_ref[start:start + t, :]
        conv_ref[0:CONV_HALO, :] = conv_ref[t:t + CONV_HALO, :]
        qk = _silu(xc)
        q_b = [head(qk, h).astype(BF16) for h in heads]
        k_f = [head(qk, HEADS + h) * (dh ** -0.5) for h in heads]
        yield
        ones_b = jnp.ones((t, dh), BF16)
        v_aug = [jnp.concatenate([head(p_ref[:, 2 * gw:3 * gw], h).astype(BF16), ones_b], axis=1)
                 for h in heads]
        li_c = [gc[:, h:h + 1] for h in heads]
        b_c = [bc_c[:, HEADS + h:HEADS + h + 1] for h in heads]
        m_prev = [m_ref[h:h + 1, 0:1] for h in heads]
        d = [jnp.where(causal, b_c[h] - bc_r[HEADS + h:HEADS + h + 1, :] + gr[h:h + 1, :], -jnp.inf)
             for h in heads]
        inter = [b_c[h] + m_prev[h] for h in heads]
        m_t = [jnp.maximum(inter[h], jnp.max(d[h], axis=-1, keepdims=True)) for h in heads]
        s = [(_dot_nt(q_b[h], k_f[h].astype(BF16)) * jnp.exp(d[h] - m_t[h])).astype(BF16)
             for h in heads]
        yield
        c_aug = [c_ref[h] for h in heads]
        out = [jnp.exp(inter[h] - m_t[h]) * _dot(q_b[h], c_aug[h].astype(BF16)) + _dot(s[h], v_aug[h])
               for h in heads]
        hm = [out[h][:, 0:dh] / jnp.maximum(jnp.abs(out[h][:, dh:2 * dh]), jnp.exp(-m_t[h]))
              for h in heads]
        yield
        for h in heads:
            b_last = b_c[h][t - 1:t, :]
            m_new = m_t[h][t - 1:t, :]
            g = jnp.exp(b_last - b_c[h] + li_c[h] - m_new)
            decay = jnp.exp(b_last + m_prev[h] - m_new)
            c_ref[h] = decay * c_aug[h] + _dot_tn((g * k_f[h]).astype(BF16), v_aug[h])
            m_ref[h:h + 1, :] = jnp.broadcast_to(m_new, (1, V7X_LANES))
        yield
        gate_m = _sigmoid(p_ref[:, 3 * gw:4 * gw])
        for h in heads:
            merged[h] = (_rms(hm[h]) * head(mg_ref[...], h) * head(gate_m, h)).astype(BF16)

    def hgrn():
        lbl = lbl_ref[...]
        e_l = jnp.exp(lbl - jnp.max(lbl, axis=0, keepdims=True))
        lb = jnp.sum(e_l[0:layer + 1, :], axis=0, keepdims=True) / jnp.sum(e_l, axis=0, keepdims=True)
        fg = lb + (1.0 - lb) * _sigmoid(p_ref[:, 5 * gw:6 * gw])
        lf = jnp.log(fg)
        lf_hi = lf.astype(BF16)
        lf_lo = (lf - lf_hi.astype(F32)).astype(BF16)
        logdec = _dot(sel_ref[...], jnp.concatenate([lf_hi, lf_lo], axis=0))
        yield
        qh = _silu(p_ref[:, 4 * gw:5 * gw])
        kh = 1.0 - fg
        qh_b = qh.astype(BF16)
        kh_b = kh.astype(BF16)
        for h in heads:
            pltpu.store(att_ref.at[h], _dot_nt(head(qh_b, h), head(kh_b, h)), mask=diag_mask)
        for l in range(LEVELS):
            if l == 0:
                z = jnp.where(upper[0], (qh * fg).astype(BF16), kh_b)
            else:
                e = jnp.exp(logdec[(l + 1) * t:(l + 2) * t, :]).astype(BF16)
                z = jnp.where(upper[l], qh_b, kh_b) * e
            for h in heads:
                pltpu.store(att_ref.at[h], _dot_nt(head(z, h), head(z, h)), mask=level_masks[l])
            yield
        bc = logdec[0:t, :]
        v_b = p_ref[:, 6 * gw:7 * gw].astype(BF16)
        st = [s_ref[h] for h in heads]
        qe = (qh * jnp.exp(bc)).astype(BF16)
        o = [_dot_nt(head(qe, h), st[h].astype(BF16)) + _dot(att_ref[h].astype(BF16), head(v_b, h))
             for h in heads]
        yield
        kd = (kh * jnp.exp(logdec[t:2 * t, :])).astype(BF16)
        dec = jnp.exp(bc[t - 1:t, :])
        for h in heads:
            s_ref[h] = head(dec, h) * st[h] + _dot_tn(head(v_b, h), head(kd, h))
        yield
        gate_h = _silu(p_ref[:, 7 * gw:8 * gw])
        for h in heads:
            merged[HEADS + h] = (_rms(o[h]) * head(hng_ref[...], h) * head(gate_h, h)).astype(BF16)

    def side():
        for task in side_tasks:
            task()
            yield

    side_stream = side()
    for main in (mlstm(), hgrn()):
        while next(main, parts_done) is not parts_done:
            next(side_stream, None)
    for _ in side_stream:
        pass
    return _dot(jnp.concatenate(merged, axis=1), wo_ref[...])


def _mixer_kernel(h_ref, hn_ref, mod_ref, modn_ref, vec_ref, wm_ref, wgc_ref, wo_ref, lbl_ref, sel_ref,
                  o_ref,
                  pa_ref, pb_ref, gca_ref, gcb_ref, yb_ref, conv_ref, c_ref, m_ref, s_ref, att_ref,
                  *, layer, steps_per_seq, row0):
    t = CHUNK
    pr = PROJ_ROWS
    i = pl.program_id(0)
    bufs = ((pa_ref, gca_ref), (pb_ref, gcb_ref))
    gw = GROUP_WIDTH
    pre_ref, post_ref, cb_ref = (vec_ref.at[r:r + 1, :] for r in (VEC_PRE, VEC_POST, VEC_CONV_B))
    mg_ref = vec_ref.at[VEC_HEAD_NORM:VEC_HEAD_NORM + 1, 0:gw]
    hng_ref = vec_ref.at[VEC_HEAD_NORM:VEC_HEAD_NORM + 1, gw:2 * gw]
    gbc_ref = vec_ref.at[VEC_GATE_B:VEC_GATE_B + 1, 0:V7X_LANES]
    cw_ref = vec_ref.at[VEC_CONV_W:VEC_CONV_W + CONV_WIDTH, :]

    def project_tasks(x, md_ref, buf):
        return _project_tasks(x, md_ref, pre_ref, wm_ref, wgc_ref, yb_ref, *buf, row0)

    @pl.when(i == 0)
    def _():
        att_ref[...] = jnp.zeros(att_ref.shape, F32)
        for task in project_tasks(h_ref[0:pr, :], mod_ref, bufs[0]):
            task()

    @pl.when(i % steps_per_seq == 0)
    def _():
        conv_ref[0:CONV_HALO, :] = jnp.zeros((CONV_HALO, 2 * GROUP_WIDTH), F32)
        c_ref[...] = jnp.zeros(c_ref.shape, F32)
        m_ref[...] = jnp.full(m_ref.shape, M_NEG, F32)
        s_ref[...] = jnp.zeros(s_ref.shape, F32)

    gate_scale = mod_ref[row0 + 2:row0 + 3, :] * post_ref[...]
    consts = _chunk_constants()
    chunks_per_proj = pr // t
    for half in range(STEP_ROWS // pr):
        p_ref, gc_ref = bufs[half % 2]
        if (half + 1) * pr < STEP_ROWS:
            side = project_tasks(h_ref[(half + 1) * pr:(half + 2) * pr, :], mod_ref, bufs[(half + 1) % 2])
        else:
            side = project_tasks(hn_ref[...], modn_ref, bufs[(half + 1) % 2])
        per_chunk = len(side) // chunks_per_proj
        for k in range(chunks_per_proj):
            rows = slice(half * pr + k * t, half * pr + (k + 1) * t)
            y = _recur_chunk(p_ref.at[k * t:(k + 1) * t, :], gc_ref.at[k * t:(k + 1) * t, :],
                             consts, cw_ref, cb_ref, gbc_ref, mg_ref, lbl_ref, hng_ref,
                             sel_ref, wo_ref, conv_ref, c_ref, m_ref, s_ref, att_ref, layer,
                             side[k * per_chunk:(k + 1) * per_chunk])
            o_ref[rows, :] = h_ref[rows, :] + gate_scale * _rms(y)


def _mixer(h, mod, pre_g, post_g, wm, wgc, wo, conv_w, conv_b, gate_b,
           mnorm_g, lb_logits, hnorm_g, *, layer, row0, seq):
    n, d = h.shape
    tm = STEP_ROWS
    pr = PROJ_ROWS
    steps = n // tm
    steps_per_seq = seq // tm
    last_proj = n // pr - 1
    gw = GROUP_WIDTH
    resident = pl.Buffered(1)
    const = lambda i: (0, 0)
    nxt_proj = lambda i: jnp.minimum((tm // pr) * (i + 1), last_proj)
    rows = {VEC_PRE: pre_g[0], VEC_POST: post_g[0], VEC_CONV_B: conv_b[0],
            VEC_HEAD_NORM: jnp.concatenate([mnorm_g[0], hnorm_g[0]]),
            VEC_GATE_B: jnp.pad(gate_b, (0, d - gate_b.shape[0]))}
    rows.update({VEC_CONV_W + j: conv_w[j] for j in range(CONV_WIDTH)})
    vecs = jnp.stack([rows.get(r, jnp.zeros((d,), F32)) for r in range(VEC_ROWS)])
    small = [lb_logits, _decay_selector()]
    est = (wm.size * 2 + wo.size * 2 + 2 * pr * PROJ_COLS * 4
           + 6 * tm * d * 4 + 16 * 1024 * 1024)
    return pl.pallas_call(
        functools.partial(_mixer_kernel, layer=layer, steps_per_seq=steps_per_seq, row0=row0),
        grid=(steps,),
        in_specs=[
            pl.BlockSpec((tm, d), lambda i: (i, 0)),
            pl.BlockSpec((pr, d), lambda i: (nxt_proj(i), 0)),
            pl.BlockSpec((None, N_MOD, d), lambda i: (i // steps_per_seq, 0, 0)),
            pl.BlockSpec((None, N_MOD, d), lambda i: (nxt_proj(i) // (seq // pr), 0, 0)),
            pl.BlockSpec(vecs.shape, const),
            pl.BlockSpec(wm.shape, lambda i: (0, 0, 0), pipeline_mode=resident),
            pl.BlockSpec(wgc.shape, const),
            pl.BlockSpec(wo.shape, const, pipeline_mode=resident),
        ] + [pl.BlockSpec(a.shape, const) for a in small],
        out_specs=pl.BlockSpec((tm, d), lambda i: (i, 0)),
        out_shape=jax.ShapeDtypeStruct((n, d), F32),
        scratch_shapes=[
            pltpu.VMEM((pr, PROJ_COLS), F32),
            pltpu.VMEM((pr, PROJ_COLS), F32),
            pltpu.VMEM((pr, V7X_LANES), F32),
            pltpu.VMEM((pr, V7X_LANES), F32),
            pltpu.VMEM((pr, d), BF16),
            pltpu.VMEM((CHUNK + CONV_HALO, 2 * gw), F32),
            pltpu.VMEM((HEADS, HEAD_DIM, 2 * HEAD_DIM), F32),
            pltpu.VMEM((M_ROWS, V7X_LANES), F32),
            pltpu.VMEM((HEADS, HEAD_DIM, HEAD_DIM), F32),
            pltpu.VMEM((HEADS, CHUNK, CHUNK), F32),
        ],
        compiler_params=pltpu.CompilerParams(
            dimension_semantics=("arbitrary",), vmem_limit_bytes=_vmem_limit(est)),
        name="mixer",
    )(h, h, mod, mod, vecs, wm, wgc, wo, *small)


def _ffn_weights(w_gate, w_up, w_down):
    d, f = w_gate.shape
    nf = f // FFN_CHUNK
    wd = w_down.reshape(nf, FFN_CHUNK, d).astype(BF16)
    return w_gate.astype(BF16), w_up.astype(BF16), wd


def kernel(x, c, ada_w, ada_b, ffn1_pre_g, ffn1_post_g, ffn1_w_gate, ffn1_w_up, ffn1_w_down, mix_pre_g, mix_post_g, w_in, mlstm_conv_w, mlstm_conv_b, mlstm_gate_b, mlstm_norm_g, hgrn_lb_logits, hgrn_norm_g, w_out, ffn2_pre_g, ffn2_post_g, ffn2_w_gate, ffn2_w_up, ffn2_w_down):
    batch, seq, d = x.shape
    n = batch * seq
    gw = GROUP_WIDTH
    depth = ada_w.shape[0]
    assert seq % FFN_TOKENS == 0 and seq % STEP_ROWS == 0 and d % V7X_LANES == 0
    assert w_in.shape[2] == PROJ_COLS + 2 * HEADS

    h = x.reshape(n, d)
    for l in range(depth):
        mod = _adaln(c, ada_w[l], ada_b[l]).reshape(batch, N_MOD, d)

        h = _ffn(h, mod, ffn1_pre_g[l:l + 1], ffn1_post_g[l:l + 1],
                 *_ffn_weights(ffn1_w_gate[l], ffn1_w_up[l], ffn1_w_down[l]), row0=0, seq=seq)

        w = w_in[l]
        w_main = jnp.concatenate([w[:, :4 * gw], w[:, 4 * gw + 2 * HEADS:]], axis=1).astype(BF16)
        w_main = w_main.reshape(d, PROJ_COLS // gw, gw).transpose(1, 0, 2)
        w_gates = w[:, 4 * gw:4 * gw + 2 * HEADS]
        w_gc = jnp.pad(w_gates, ((0, 0), (0, V7X_LANES - 2 * HEADS))).astype(BF16)
        h = _mixer(h, mod, mix_pre_g[l:l + 1], mix_post_g[l:l + 1], w_main, w_gc,
                   w_out[l].astype(BF16), mlstm_conv_w[l], mlstm_conv_b[l:l + 1], mlstm_gate_b[l],
                   mlstm_norm_g[l:l + 1], hgrn_lb_logits, hgrn_norm_g[l:l + 1],
                   layer=l, row0=3, seq=seq)

        h = _ffn(h, mod, ffn2_pre_g[l:l + 1], ffn2_post_g[l:l + 1],
                 *_ffn_weights(ffn2_w_gate[l], ffn2_w_up[l], ffn2_w_down[l]), row0=6, seq=seq)
    return h.reshape(batch, seq, d)
```

```python
import functools

import numpy as np
import jax
import jax.numpy as jnp
from jax import lax
from jax.experimental import pallas as pl
from jax.experimental.pallas import tpu as pltpu

F32 = jnp.float32
BF16 = jnp.bfloat16

V7X_LANES = 128
V7X_SUBLANES = 8
V7X_MXU_WIDTH = 256
V7X_VMEM_BYTES = 64 * 1024 * 1024

EPS = 1e-6
M_NEG = -1e30
N_MOD = 9
HEADS = 4
HEAD_DIM = 128
GROUP_WIDTH = HEADS * HEAD_DIM
CONV_WIDTH = 4
FFN_RESIDUAL_WEIGHT = 0.5

FFN_TOKENS = 1024
FFN_PIECES = 4
FFN_CHUNK = V7X_MXU_WIDTH
CHUNK = 128
PROJ_ROWS = 2 * CHUNK
STEP_ROWS = 2 * PROJ_ROWS
LEVELS = 7
PROJ_COLS = 8 * GROUP_WIDTH
CONV_HALO = V7X_SUBLANES
VEC_PRE, VEC_POST, VEC_CONV_B, VEC_HEAD_NORM, VEC_GATE_B, VEC_CONV_W = (r * V7X_SUBLANES for r in range(6))
VEC_ROWS = 8 * V7X_SUBLANES
M_ROWS = 4 * V7X_SUBLANES


def _sigmoid(x):
    return 1.0 / (1.0 + jnp.exp(-x))


def _silu(x):
    return x * _sigmoid(x)


def _log_sigmoid(x):
    return jnp.minimum(x, 0.0) - jnp.log(1.0 + jnp.exp(-jnp.abs(x)))


def _rms(x):
    return x * lax.rsqrt(jnp.mean(x * x, axis=-1, keepdims=True) + EPS)


def _dot(a, b):
    return jnp.dot(a, b, preferred_element_type=F32)


def _dot_nt(a, b):
    return lax.dot_general(a, b, (((1,), (1,)), ((), ())), preferred_element_type=F32)


def _dot_tn(a, b):
    return lax.dot_general(a, b, (((0,), (0,)), ((), ())), preferred_element_type=F32)


def _split3(x):
    p1 = x.astype(BF16)
    r1 = x - p1.astype(F32)
    p2 = r1.astype(BF16)
    p3 = (r1 - p2.astype(F32)).astype(BF16)
    return p1, p2, p3


def _vmem_limit(nbytes):
    return int(min(nbytes, V7X_VMEM_BYTES - 4 * 1024 * 1024))


def _mod_kernel(c_ref, w_ref, b_ref, o_ref):
    sc = _silu(c_ref[...]).astype(BF16)
    o_ref[...] = _dot(sc, w_ref[...].astype(BF16)) + b_ref[...]


def _adaln(c, w, b):
    bsz, d = c.shape
    n = w.shape[1]
    return pl.pallas_call(
        _mod_kernel,
        grid=(n // d,),
        in_specs=[
            pl.BlockSpec((bsz, d), lambda j: (0, 0)),
            pl.BlockSpec((d, d), lambda j: (0, j)),
            pl.BlockSpec((1, d), lambda j: (0, j)),
        ],
        out_specs=pl.BlockSpec((bsz, d), lambda j: (0, j)),
        out_shape=jax.ShapeDtypeStruct((bsz, n), F32),
        name="adaln_mod",
    )(c, w, b.reshape(1, n))


def _ffn_kernel(x_ref, mod_ref, pre_ref, post_ref, wg_ref, wu_ref, wd_ref, o_ref, yb_ref, acc_ref, *, row0):
    tm = x_ref.shape[0]
    half = tm // 2
    piece = half // FFN_PIECES
    in_scale = pre_ref[...] * (1.0 + mod_ref[row0 + 1:row0 + 2, :])
    in_shift = mod_ref[row0:row0 + 1, :]
    out_scale = (FFN_RESIDUAL_WEIGHT * mod_ref[row0 + 2:row0 + 3, :]) * post_ref[...]

    def prologue(r):
        rows = slice(r, r + piece)
        yb_ref[rows, :] = (_rms(x_ref[rows, :]) * in_scale + in_shift).astype(BF16)

    def epilogue(r):
        rows = slice(r, r + piece)
        o_ref[rows, :] = x_ref[rows, :] + out_scale * _rms(acc_ref[rows, :])

    def swiglu(r, side_tasks):
        side_tasks = list(side_tasks)
        yb = yb_ref[r:r + half, :]
        acc = None
        for c in range(wd_ref.shape[0]):
            cols = slice(c * FFN_CHUNK, (c + 1) * FFN_CHUNK)
            g = _dot(yb, wg_ref[:, cols])
            u = _dot(yb, wu_ref[:, cols])
            part = _dot((_silu(g) * u).astype(BF16), wd_ref[c])
            acc = part if acc is None else acc + part
            if side_tasks:
                side_tasks.pop(0)()
        acc_ref[r:r + half, :] = acc

    pieces = [k * piece for k in range(FFN_PIECES)]
    for r in pieces:
        prologue(r)
    swiglu(0, [functools.partial(prologue, half + r) for r in pieces])
    swiglu(half, [functools.partial(epilogue, r) for r in pieces])
    for r in pieces:
        epilogue(half + r)


def _ffn(h, mod, pre_g, post_g, wg, wu, wd, *, row0, seq):
    n, d = h.shape
    tm = FFN_TOKENS
    tiles_per_seq = seq // tm
    nf, fc, _ = wd.shape
    const3 = lambda i: (0, 0, 0)
    resident = pl.Buffered(1)
    est = (4 * tm * d * 4
           + 3 * nf * d * fc * 2
           + tm * d * (2 + 4)
           + 3 * tm * d * 4)
    return pl.pallas_call(
        functools.partial(_ffn_kernel, row0=row0),
        grid=(n // tm,),
        in_specs=[
            pl.BlockSpec((tm, d), lambda i: (i, 0)),
            pl.BlockSpec((None, N_MOD, d), lambda i: (i // tiles_per_seq, 0, 0)),
            pl.BlockSpec((1, d), lambda i: (0, 0)),
            pl.BlockSpec((1, d), lambda i: (0, 0)),
            pl.BlockSpec(wg.shape, lambda i: (0, 0), pipeline_mode=resident),
            pl.BlockSpec(wu.shape, lambda i: (0, 0), pipeline_mode=resident),
            pl.BlockSpec((nf, fc, d), const3, pipeline_mode=resident),
        ],
        out_specs=pl.BlockSpec((tm, d), lambda i: (i, 0)),
        out_shape=jax.ShapeDtypeStruct((n, d), F32),
        scratch_shapes=[pltpu.VMEM((tm, d), BF16),
                        pltpu.VMEM((tm, d), F32)],
        compiler_params=pltpu.CompilerParams(
            dimension_semantics=("arbitrary",), vmem_limit_bytes=_vmem_limit(est)),
        name="ffn",
    )(h, mod, pre_g, post_g, wg, wu, wd)


def _project_tasks(x, mod_ref, pre_ref, wm_ref, wgc_ref, yb_ref, p_ref, gc_ref, row0):
    w = GROUP_WIDTH

    def piece(c):
        def run():
            if c == 0:
                y = _rms(x) * pre_ref[...]
                y = y * (1.0 + mod_ref[row0 + 1:row0 + 2, :]) + mod_ref[row0:row0 + 1, :]
                yb_ref[...] = y.astype(BF16)
            yb = yb_ref[...]
            p_ref[:, c * w:(c + 1) * w] = _dot(yb, wm_ref[c])
            if c == PROJ_COLS // w - 1:
                gc_ref[...] = _dot(yb, wgc_ref[...])
        return run

    return [piece(c) for c in range(PROJ_COLS // w)]


def _decay_selector():
    t = CHUNK
    tt, rr = np.meshgrid(np.arange(t), np.arange(t), indexing="ij")
    blocks = [rr <= tt, rr > tt]
    for l in range(1, LEVELS):
        half = 1 << l
        anchor = (tt & ~(2 * half - 1)) + half - 1
        is_upper = ((tt >> l) & 1) == 1
        blocks.append(np.where(is_upper, (rr > anchor) & (rr <= tt), (rr > tt) & (rr <= anchor)))
    sel = np.concatenate(blocks, axis=0).astype(np.float32)
    return jnp.asarray(np.concatenate([sel, sel], axis=1), dtype=BF16)


def _chunk_constants():
    t = CHUNK
    row = lax.broadcasted_iota(jnp.int32, (t, t), 0)
    col = lax.broadcasted_iota(jnp.int32, (t, t), 1)
    causal = col <= row
    tri_l = jnp.where(causal, 1.0, 0.0).astype(BF16)
    xor = row ^ col
    level_masks = [(col < row) & ((xor >> l) == 1) for l in range(LEVELS)]
    diag_mask = row == col
    t_col = lax.broadcasted_iota(jnp.int32, (t, 1), 0)
    upper = [((t_col >> l) & 1) == 1 for l in range(LEVELS)]
    return causal, tri_l, level_masks, diag_mask, upper


def _recur_chunk(p_ref, gc_ref, consts, cw_ref, cb_ref, gbc_ref, mg_ref, lbl_ref, hng_ref,
                 sel_ref, wo_ref, conv_ref, c_ref, m_ref, s_ref, att_ref, layer, side_tasks):
    t = CHUNK
    dh = HEAD_DIM
    gw = GROUP_WIDTH
    heads = range(HEADS)
    causal, tri_l, level_masks, diag_mask, upper = consts
    merged = [None] * (2 * HEADS)
    parts_done = object()

    def head(x, h):
        return x[:, h * dh:(h + 1) * dh]

    def mlstm():
        gc = gc_ref[...] + gbc_ref[...]
        bc_c = sum(_dot(tri_l, part) for part in _split3(_log_sigmoid(gc)))
        gr = gc.T
        bc_r = bc_c.T
        yield
        conv_ref[CONV_HALO:CONV_HALO + t, :] = p_ref[:, 0:2 * gw]
        xc = cb_ref[...]
        for j in range(CONV_WIDTH):
            start = CONV_HALO - (CONV_WIDTH - 1) + j
            xc = xc + cw_ref[j:j + 1, :] * conv_ref[start:start + t, :]
        conv_ref[0:CONV_HALO, :] = conv_ref[t:t + CONV_HALO, :]
        qk = _silu(xc)
        q_b = [head(qk, h).astype(BF16) for h in heads]
        k_f = [head(qk, HEADS + h) * (dh ** -0.5) for h in heads]
        yield
        ones_b = jnp.ones((t, dh), BF16)
        v_aug = [jnp.concatenate([head(p_ref[:, 2 * gw:3 * gw], h).astype(BF16), ones_b], axis=1)
                 for h in heads]
        li_c = [gc[:, h:h + 1] for h in heads]
        b_c = [bc_c[:, HEADS + h:HEADS + h + 1] for h in heads]
        m_prev = [m_ref[h:h + 1, 0:1] for h in heads]
        d = [jnp.where(causal, b_c[h] - bc_r[HEADS + h:HEADS + h + 1, :] + gr[h:h + 1, :], -jnp.inf)
             for h in heads]
        inter = [b_c[h] + m_prev[h] for h in heads]
        m_t = [jnp.maximum(inter[h], jnp.max(d[h], axis=-1, keepdims=True)) for h in heads]
        s = [(_dot_nt(q_b[h], k_f[h].astype(BF16)) * jnp.exp(d[h] - m_t[h])).astype(BF16)
             for h in heads]
        yield
        c_aug = [c_ref[h] for h in heads]
        out = [jnp.exp(inter[h] - m_t[h]) * _dot(q_b[h], c_aug[h].astype(BF16)) + _dot(s[h], v_aug[h])
               for h in heads]
        hm = [out[h][:, 0:dh] / jnp.maximum(jnp.abs(out[h][:, dh:2 * dh]), jnp.exp(-m_t[h]))
              for h in heads]
        yield
        for h in heads:
            b_last = b_c[h][t - 1:t, :]
            m_new = m_t[h][t - 1:t, :]
            g = jnp.exp(b_last - b_c[h] + li_c[h] - m_new)
            decay = jnp.exp(b_last + m_prev[h] - m_new)
            c_ref[h] = decay * c_aug[h] + _dot_tn((g * k_f[h]).astype(BF16), v_aug[h])
            m_ref[h:h + 1, :] = jnp.broadcast_to(m_new, (1, V7X_LANES))
        yield
        gate_m = _sigmoid(p_ref[:, 3 * gw:4 * gw])
        for h in heads:
            merged[h] = (_rms(hm[h]) * head(mg_ref[...], h) * head(gate_m, h)).astype(BF16)

    def hgrn():
        lbl = lbl_ref[...]
        e_l = jnp.exp(lbl - jnp.max(lbl, axis=0, keepdims=True))
        lb = jnp.sum(e_l[0:layer + 1, :], axis=0, keepdims=True) / jnp.sum(e_l, axis=0, keepdims=True)
        fg = lb + (1.0 - lb) * _sigmoid(p_ref[:, 5 * gw:6 * gw])
        lf = jnp.log(fg)
        lf_hi = lf.astype(BF16)
        lf_lo = (lf - lf_hi.astype(F32)).astype(BF16)
        logdec = _dot(sel_ref[...], jnp.concatenate([lf_hi, lf_lo], axis=0))
        yield
        qh = _silu(p_ref[:, 4 * gw:5 * gw])
        kh = 1.0 - fg
        qh_b = qh.astype(BF16)
        kh_b = kh.astype(BF16)
        for h in heads:
            pltpu.store(att_ref.at[h], _dot_nt(head(qh_b, h), head(kh_b, h)), mask=diag_mask)
        for l in range(LEVELS):
            if l == 0:
                z = jnp.where(upper[0], (qh * fg).astype(BF16), kh_b)
            else:
                e = jnp.exp(logdec[(l + 1) * t:(l + 2) * t, :]).astype(BF16)
                z = jnp.where(upper[l], qh_b, kh_b) * e
            for h in heads:
                pltpu.store(att_ref.at[h], _dot_nt(head(z, h), head(z, h)), mask=level_masks[l])
            yield
        bc = logdec[0:t, :]
        v_b = p_ref[:, 6 * gw:7 * gw].astype(BF16)
        st = [s_ref[h] for h in heads]
        qe = (qh * jnp.exp(bc)).astype(BF16)
        o = [_dot_nt(head(qe, h), st[h].astype(BF16)) + _dot(att_ref[h].astype(BF16), head(v_b, h))
             for h in heads]
        yield
        kd = (kh * jnp.exp(logdec[t:2 * t, :])).astype(BF16)
        dec = jnp.exp(bc[t - 1:t, :])
        for h in heads:
            s_ref[h] = head(dec, h) * st[h] + _dot_tn(head(v_b, h), head(kd, h))
        yield
        gate_h = _silu(p_ref[:, 7 * gw:8 * gw])
        for h in heads:
            merged[HEADS + h] = (_rms(o[h]) * head(hng_ref[...], h) * head(gate_h, h)).astype(BF16)

    def side():
        for task in side_tasks:
            task()
            yield

    side_stream = side()
    hgrn_stream = hgrn()
    next(hgrn_stream)
    for main in (mlstm(), hgrn_stream):
        while next(main, parts_done) is not parts_done:
            next(side_stream, None)
    for _ in side_stream:
        pass
    return _dot(jnp.concatenate(merged, axis=1), wo_ref[...])


def _mixer_kernel(h_ref, hn_ref, mod_ref, modn_ref, vec_ref, wm_ref, wgc_ref, wo_ref, lbl_ref, sel_ref,
                  o_ref,
                  pa_ref, pb_ref, gca_ref, gcb_ref, yb_ref, conv_ref, c_ref, m_ref, s_ref, att_ref,
                  *, layer, steps_per_seq, row0):
    t = CHUNK
    pr = PROJ_ROWS
    i = pl.program_id(0)
    bufs = ((pa_ref, gca_ref), (pb_ref, gcb_ref))
    gw = GROUP_WIDTH
    pre_ref, post_ref, cb_ref = (vec_ref.at[r:r + 1, :] for r in (VEC_PRE, VEC_POST, VEC_CONV_B))
    mg_ref = vec_ref.at[VEC_HEAD_NORM:VEC_HEAD_NORM + 1, 0:gw]
    hng_ref = vec_ref.at[VEC_HEAD_NORM:VEC_HEAD_NORM + 1, gw:2 * gw]
    gbc_ref = vec_ref.at[VEC_GATE_B:VEC_GATE_B + 1, 0:V7X_LANES]
    cw_ref = vec_ref.at[VEC_CONV_W:VEC_CONV_W + CONV_WIDTH, :]

    def project_tasks(x, md_ref, buf):
        return _project_tasks(x, md_ref, pre_ref, wm_ref, wgc_ref, yb_ref, *buf, row0)

    @pl.when(i == 0)
    def _():
        att_ref[...] = jnp.zeros(att_ref.shape, F32)
        for task in project_tasks(h_ref[0:pr, :], mod_ref, bufs[0]):
            task()

    @pl.when(i % steps_per_seq == 0)
    def _():
        conv_ref[0:CONV_HALO, :] = jnp.zeros((CONV_HALO, 2 * GROUP_WIDTH), F32)
        c_ref[...] = jnp.zeros(c_ref.shape, F32)
        m_ref[...] = jnp.full(m_ref.shape, M_NEG, F32)
        s_ref[...] = jnp.zeros(s_ref.shape, F32)

    gate_scale = mod_ref[row0 + 2:row0 + 3, :] * post_ref[...]
    consts = _chunk_constants()
    chunks_per_proj = pr // t
    for half in range(STEP_ROWS // pr):
        p_ref, gc_ref = bufs[half % 2]
        if (half + 1) * pr < STEP_ROWS:
            side = project_tasks(h_ref[(half + 1) * pr:(half + 2) * pr, :], mod_ref, bufs[(half + 1) % 2])
        else:
            side = project_tasks(hn_ref[...], modn_ref, bufs[(half + 1) % 2])
        per_chunk = len(side) // chunks_per_proj
        for k in range(chunks_per_proj):
            rows = slice(half * pr + k * t, half * pr + (k + 1) * t)
            y = _recur_chunk(p_ref.at[k * t:(k + 1) * t, :], gc_ref.at[k * t:(k + 1) * t, :],
                             consts, cw_ref, cb_ref, gbc_ref, mg_ref, lbl_ref, hng_ref,
                             sel_ref, wo_ref, conv_ref, c_ref, m_ref, s_ref, att_ref, layer,
                             side[k * per_chunk:(k + 1) * per_chunk])
            o_ref[rows, :] = h_ref[rows, :] + gate_scale * _rms(y)


def _mixer(h, mod, pre_g, post_g, wm, wgc, wo, conv_w, conv_b, gate_b,
           mnorm_g, lb_logits, hnorm_g, *, layer, row0, seq):
    n, d = h.shape
    tm = STEP_ROWS
    pr = PROJ_ROWS
    steps = n // tm
    steps_per_seq = seq // tm
    last_proj = n // pr - 1
    gw = GROUP_WIDTH
    resident = pl.Buffered(1)
    const = lambda i: (0, 0)
    nxt_proj = lambda i: jnp.minimum((tm // pr) * (i + 1), last_proj)
    rows = {VEC_PRE: pre_g[0], VEC_POST: post_g[0], VEC_CONV_B: conv_b[0],
            VEC_HEAD_NORM: jnp.concatenate([mnorm_g[0], hnorm_g[0]]),
            VEC_GATE_B: jnp.pad(gate_b, (0, d - gate_b.shape[0]))}
    rows.update({VEC_CONV_W + j: conv_w[j] for j in range(CONV_WIDTH)})
    vecs = jnp.stack([rows.get(r, jnp.zeros((d,), F32)) for r in range(VEC_ROWS)])
    small = [lb_logits, _decay_selector()]
    est = (wm.size * 2 + wo.size * 2 + 2 * pr * PROJ_COLS * 4
           + 6 * tm * d * 4 + 16 * 1024 * 1024)
    return pl.pallas_call(
        functools.partial(_mixer_kernel, layer=layer, steps_per_seq=steps_per_seq, row0=row0),
        grid=(steps,),
        in_specs=[
            pl.BlockSpec((tm, d), lambda i: (i, 0)),
            pl.BlockSpec((pr, d), lambda i: (nxt_proj(i), 0)),
            pl.BlockSpec((None, N_MOD, d), lambda i: (i // steps_per_seq, 0, 0)),
            pl.BlockSpec((None, N_MOD, d), lambda i: (nxt_proj(i) // (seq // pr), 0, 0)),
            pl.BlockSpec(vecs.shape, const),
            pl.BlockSpec(wm.shape, lambda i: (0, 0, 0), pipeline_mode=resident),
            pl.BlockSpec(wgc.shape, const),
            pl.BlockSpec(wo.shape, const, pipeline_mode=resident),
        ] + [pl.BlockSpec(a.shape, const) for a in small],
        out_specs=pl.BlockSpec((tm, d), lambda i: (i, 0)),
        out_shape=jax.ShapeDtypeStruct((n, d), F32),
        scratch_shapes=[
            pltpu.VMEM((pr, PROJ_COLS), F32),
            pltpu.VMEM((pr, PROJ_COLS), F32),
            pltpu.VMEM((pr, V7X_LANES), F32),
            pltpu.VMEM((pr, V7X_LANES), F32),
            pltpu.VMEM((pr, d), BF16),
            pltpu.VMEM((CHUNK + CONV_HALO, 2 * gw), F32),
            pltpu.VMEM((HEADS, HEAD_DIM, 2 * HEAD_DIM), F32),
            pltpu.VMEM((M_ROWS, V7X_LANES), F32),
            pltpu.VMEM((HEADS, HEAD_DIM, HEAD_DIM), F32),
            pltpu.VMEM((HEADS, CHUNK, CHUNK), F32),
        ],
        compiler_params=pltpu.CompilerParams(
            dimension_semantics=("arbitrary",), vmem_limit_bytes=_vmem_limit(est)),
        name="mixer",
    )(h, h, mod, mod, vecs, wm, wgc, wo, *small)


def _ffn_weights(w_gate, w_up, w_down):
    d, f = w_gate.shape
    nf = f // FFN_CHUNK
    wd = w_down.reshape(nf, FFN_CHUNK, d).astype(BF16)
    return w_gate.astype(BF16), w_up.astype(BF16), wd


def kernel(x, c, ada_w, ada_b, ffn1_pre_g, ffn1_post_g, ffn1_w_gate, ffn1_w_up, ffn1_w_down, mix_pre_g, mix_post_g, w_in, mlstm_conv_w, mlstm_conv_b, mlstm_gate_b, mlstm_norm_g, hgrn_lb_logits, hgrn_norm_g, w_out, ffn2_pre_g, ffn2_post_g, ffn2_w_gate, ffn2_w_up, ffn2_w_down):
    batch, seq, d = x.shape
    n = batch * seq
    gw = GROUP_WIDTH
    depth = ada_w.shape[0]
    assert seq % FFN_TOKENS == 0 and seq % STEP_ROWS == 0 and d % V7X_LANES == 0
    assert w_in.shape[2] == PROJ_COLS + 2 * HEADS

    h = x.reshape(n, d)
    for l in range(depth):
        mod = _adaln(c, ada_w[l], ada_b[l]).reshape(batch, N_MOD, d)

        h = _ffn(h, mod, ffn1_pre_g[l:l + 1], ffn1_post_g[l:l + 1],
                 *_ffn_weights(ffn1_w_gate[l], ffn1_w_up[l], ffn1_w_down[l]), row0=0, seq=seq)

        w = w_in[l]
        w_main = jnp.concatenate([w[:, :4 * gw], w[:, 4 * gw + 2 * HEADS:]], axis=1).astype(BF16)
        w_main = w_main.reshape(d, PROJ_COLS // gw, gw).transpose(1, 0, 2)
        w_gates = w[:, 4 * gw:4 * gw + 2 * HEADS]
        w_gc = jnp.pad(w_gates, ((0, 0), (0, V7X_LANES - 2 * HEADS))).astype(BF16)
        h = _mixer(h, mod, mix_pre_g[l:l + 1], mix_post_g[l:l + 1], w_main, w_gc,
                   w_out[l].astype(BF16), mlstm_conv_w[l], mlstm_conv_b[l:l + 1], mlstm_gate_b[l],
                   mlstm_norm_g[l:l + 1], hgrn_lb_logits, hgrn_norm_g[l:l + 1],
                   layer=l, row0=3, seq=seq)

        h = _ffn(h, mod, ffn2_pre_g[l:l + 1], ffn2_post_g[l:l + 1],
                 *_ffn_weights(ffn2_w_gate[l], ffn2_w_up[l], ffn2_w_down[l]), row0=6, seq=seq)
    return h.reshape(batch, seq, d)
```
